```python
import math
import jax, jax.numpy as jnp
from jax import lax
import numpy as np

D_MODEL = 1024
BATCH = 8
SEQ = 2048
DEPTH = 2

N_A = DEPTH // 2
N_B = DEPTH - N_A
CONV_WIDTH = 3
N_HEADS = 8
HEAD_DIM = 64
V_DIM = 2 * HEAD_DIM
D_FF = 4 * D_MODEL
Q_BLOCK = 128
EPS = 1e-6
NEG_INF = -1e30

kernel_name = "yoco_shortconv_diffattn_trunk"


def rms_norm(x, g):
    xf = x.astype(jnp.float32)
    y = xf * lax.rsqrt(jnp.mean(xf * xf, axis=-1, keepdims=True) + EPS)
    return (y * g.astype(jnp.float32)).astype(x.dtype)


def lambda_init_fn(layer_idx):
    return 0.8 - 0.6 * math.exp(-0.3 * layer_idx)


def short_conv_mixer(h, w_in, conv_w, w_out):
    bcv = h @ w_in
    b_gate, c_gate, v = jnp.split(bcv, 3, axis=-1)
    u = c_gate * v
    u = lax.conv_general_dilated(
        u, conv_w[:, None, :].astype(u.dtype),
        window_strides=(1,), padding=[(CONV_WIDTH - 1, 0)],
        dimension_numbers=("NWC", "WIO", "NWC"),
        feature_group_count=D_MODEL)
    return (b_gate * u) @ w_out


def squared_relu_mlp(h, w_up, w_down):
    a = jax.nn.relu(h @ w_up)
    return (a * a) @ w_down


def shared_kv(x, kv_norm, w_kv, k_norm):
    b, s, _ = x.shape
    kv = rms_norm(x, kv_norm) @ w_kv
    k = kv[..., :N_HEADS * 2 * HEAD_DIM].reshape(b, s, N_HEADS, 2, HEAD_DIM)
    k = rms_norm(k, k_norm)
    v = kv[..., N_HEADS * 2 * HEAD_DIM:].reshape(b, s, N_HEADS, V_DIM)
    return k, v


def diff_attention(h, k, v, w_q, q_norm, lq1, lk1, lq2, lk2, sub_norm, w_o, lam_init):
    b, s, _ = h.shape
    q = (h @ w_q).reshape(b, s, N_HEADS, 2, HEAD_DIM)
    q = rms_norm(q, q_norm)
    lam = (jnp.exp(jnp.sum(lq1.astype(jnp.float32) * lk1.astype(jnp.float32)))
           - jnp.exp(jnp.sum(lq2.astype(jnp.float32) * lk2.astype(jnp.float32)))
           + lam_init)
    scale = HEAD_DIM ** -0.5
    outs = []
    for i in range(s // Q_BLOCK):
        q_end = (i + 1) * Q_BLOCK
        qb = q[:, i * Q_BLOCK:q_end]
        kb = k[:, :q_end]
        vb = v[:, :q_end]
        sc = jnp.einsum('bqhcd,bkhcd->bhcqk', qb, kb,
                        preferred_element_type=jnp.float32) * scale
        q_pos = i * Q_BLOCK + jnp.arange(Q_BLOCK)
        k_pos = jnp.arange(q_end)
        causal = k_pos[None, :] <= q_pos[:, None]
        p = jax.nn.softmax(jnp.where(causal, sc, NEG_INF), axis=-1)
        a = p[:, :, 0] - lam * p[:, :, 1]
        outs.append(jnp.einsum('bhqk,bkhe->bqhe', a.astype(vb.dtype), vb))
    o = jnp.concatenate(outs, axis=1)
    o = rms_norm(o, sub_norm) * (1.0 - lam_init)
    return o.reshape(b, s, N_HEADS * V_DIM) @ w_o


def setup_inputs(seed: int = 0) -> dict:
    key = jax.random.key(seed)
    ks = jax.random.split(key, 24)
    f32 = jnp.float32

    def nrm(k, shape, scale):
        return jax.random.normal(k, shape, f32) * scale

    def gain(k, shape):
        return 1.0 + 0.02 * jax.random.normal(k, shape, f32)

    D = D_MODEL
    return {
        "x": jax.random.normal(ks[0], (BATCH, SEQ, D), f32),
        "a_norm": gain(ks[1], (N_A, D)),
        "a_w_in": nrm(ks[2], (N_A, D, 3 * D), D ** -0.5),
        "a_conv": nrm(ks[3], (N_A, CONV_WIDTH, D), CONV_WIDTH ** -0.5),
        "a_w_out": nrm(ks[4], (N_A, D, D), D ** -0.5),
        "kv_norm": gain(ks[5], (D,)),
        "w_kv": nrm(ks[6], (D, N_HEADS * 2 * HEAD_DIM + N_HEADS * V_DIM), D ** -0.5),
        "k_norm": gain(ks[7], (HEAD_DIM,)),
        "b_norm": gain(ks[8], (N_B, D)),
        "w_q": nrm(ks[9], (N_B, D, N_HEADS * 2 * HEAD_DIM), D ** -0.5),
        "q_norm": gain(ks[10], (N_B, HEAD_DIM)),
        "lam_q1": nrm(ks[11], (N_B, HEAD_DIM), 0.1),
        "lam_k1": nrm(ks[12], (N_B, HEAD_DIM), 0.1),
        "lam_q2": nrm(ks[13], (N_B, HEAD_DIM), 0.1),
        "lam_k2": nrm(ks[14], (N_B, HEAD_DIM), 0.1),
        "sub_norm": gain(ks[15], (N_B, V_DIM)),
        "w_o": nrm(ks[16], (N_B, N_HEADS * V_DIM, D), (N_HEADS * V_DIM) ** -0.5),
        "mlp_norm": gain(ks[17], (DEPTH, D)),
        "w_up": nrm(ks[18], (DEPTH, D, D_FF), D ** -0.5),
        "w_down": nrm(ks[19], (DEPTH, D_FF, D), D_FF ** -0.5),
    }


def reference(x, a_norm, a_w_in, a_conv, a_w_out, kv_norm, w_kv, k_norm,
              b_norm, w_q, q_norm, lam_q1, lam_k1, lam_q2, lam_k2, sub_norm,
              w_o, mlp_norm, w_up, w_down):
    k_sh = None
    v_sh = None
    for l in range(DEPTH):
        if l < N_A:
            x = x + short_conv_mixer(rms_norm(x, a_norm[l]), a_w_in[l],
                                     a_conv[l], a_w_out[l])
        else:
            j = l - N_A
            x = x + diff_attention(rms_norm(x, b_norm[j]), k_sh, v_sh, w_q[j],
                                   q_norm[j], lam_q1[j], lam_k1[j], lam_q2[j],
                                   lam_k2[j], sub_norm[j], w_o[j],
                                   lambda_init_fn(l))
        x = x + squared_relu_mlp(rms_norm(x, mlp_norm[l]), w_up[l], w_down[l])
        if l == N_A - 1:
            k_sh, v_sh = shared_kv(x, kv_norm, w_kv, k_norm)
    return x
```

```python
import functools
import math

import jax
import jax.numpy as jnp
from jax import lax
from jax.experimental import pallas as pl
from jax.experimental.pallas import tpu as pltpu

EPS = 1e-6
NEG_INF = -1e30
HEAD_DIM = 64
V_DIM = 2 * HEAD_DIM
CONV_WIDTH = 3

V7X_LANES = 128
V7X_SUBLANES = 8
V7X_MXU_DIM = 256
V7X_VMEM_LIMIT_BYTES = 60 * 1024 * 1024

ROW_TILE = 512
FF_CHUNK = 1024
Q_TILE = 256
K_TILE = 256

_BF16 = jnp.bfloat16
_F32 = jnp.float32


def _dot(a, b):
    return jnp.dot(a, b, preferred_element_type=_F32)


def _dot_nt(a, b):
    return lax.dot_general(a, b, (((1,), (1,)), ((), ())),
                           preferred_element_type=_F32)


def _rms(x, g):
    ms = jnp.mean(x * x, axis=-1, keepdims=True)
    return x * lax.rsqrt(ms + EPS) * g


def _mlp(x, g, wup_ref, wdn_ref):
    h = _rms(x, g).astype(_BF16)
    acc = x
    d_ff = wup_ref.shape[1]
    for c in range(d_ff // FF_CHUNK):
        sl = slice(c * FF_CHUNK, (c + 1) * FF_CHUNK)
        a = jnp.maximum(_dot(h, wup_ref[:, sl]), 0.0)
        acc = acc + _dot((a * a).astype(_BF16), wdn_ref[sl, :])
    return acc


def _head_rms(y, ones_ref, gain):
    y2 = (y * y).astype(_BF16)
    n = y.shape[1]
    parts = [_dot(y2[:, t:t + V7X_MXU_DIM], ones_ref[...])
             for t in range(0, n, V7X_MXU_DIM)]
    ss = jnp.concatenate(parts, axis=1)
    return y * lax.rsqrt(ss * (1.0 / HEAD_DIM) + EPS) * gain


def _mixer_kernel(x_ref, g_ref, win_ref, conv_ref, wout_ref, o_ref, carry_ref):
    d = x_ref.shape[2]
    ts = x_ref.shape[1]

    @pl.when(pl.program_id(1) == 0)
    def _():
        carry_ref[...] = jnp.zeros_like(carry_ref)

    x = x_ref[0]
    h = _rms(x, g_ref[...]).astype(_BF16)
    b_gate = _dot(h, win_ref[:, 0:d])
    c_gate = _dot(h, win_ref[:, d:2 * d])
    v = _dot(h, win_ref[:, 2 * d:3 * d])
    u = c_gate * v

    rows = lax.broadcasted_iota(jnp.int32, u.shape, 0)
    tail = carry_ref[...]
    prev1 = tail[V7X_SUBLANES - 1:V7X_SUBLANES]
    prev2 = tail[V7X_SUBLANES - 2:V7X_SUBLANES - 1]
    u1 = jnp.where(rows == 0, prev1, pltpu.roll(u, 1, 0))
    u2 = jnp.where(rows == 0, prev2,
                   jnp.where(rows == 1, prev1, pltpu.roll(u, 2, 0)))
    w = conv_ref[...]
    y = w[0:1] * u2 + w[1:2] * u1 + w[2:3] * u
    carry_ref[...] = u[ts - V7X_SUBLANES:ts]

    o_ref[0] = x + _dot((b_gate * y).astype(_BF16), wout_ref[...])


def _const_spec(shape):
    return pl.BlockSpec(shape, lambda *_: (0,) * len(shape),
                        pipeline_mode=pl.Buffered(1))


def _mixer(x, g, w_in, conv_w, w_out):
    b, s, d = x.shape
    ts = ROW_TILE
    assert s % ts == 0
    tok = pl.BlockSpec((1, ts, d), lambda i, j: (i, j, 0))
    return pl.pallas_call(
        _mixer_kernel,
        grid=(b, s // ts),
        in_specs=[tok, _const_spec((1, d)), _const_spec((d, 3 * d)),
                  _const_spec((CONV_WIDTH, d)), _const_spec((d, d))],
        out_specs=tok,
        out_shape=jax.ShapeDtypeStruct(x.shape, _F32),
        scratch_shapes=[pltpu.VMEM((V7X_SUBLANES, d), _F32)],
        compiler_params=pltpu.CompilerParams(
            dimension_semantics=("arbitrary", "arbitrary"),
            vmem_limit_bytes=V7X_VMEM_LIMIT_BYTES),
        name="mixer",
    )(x, g, w_in, conv_w, w_out)


def _mlp_qkv_kernel(x_ref, gm_ref, wup_ref, wdn_ref, gkv_ref, wkv_ref, gk_ref,
                    gb_ref, wq_ref, gq_ref, ones_ref,
                    x1_ref, q_ref, k_ref, v_ref):
    x1 = _mlp(x_ref[...], gm_ref[...], wup_ref, wdn_ref)
    x1_ref[...] = x1

    nk = k_ref.shape[1]
    hkv = _rms(x1, gkv_ref[...]).astype(_BF16)
    k = _dot(hkv, wkv_ref[:, 0:nk])
    k_ref[...] = _head_rms(k, ones_ref, gk_ref[...]).astype(_BF16)
    v_ref[...] = _dot(hkv, wkv_ref[:, nk:]).astype(_BF16)

    hq = _rms(x1, gb_ref[...]).astype(_BF16)
    q = _head_rms(_dot(hq, wq_ref[...]), ones_ref, gq_ref[...])
    q_ref[...] = (q * (HEAD_DIM ** -0.5)).astype(_BF16)


def _mlp_qkv(x, g_mlp, w_up, w_down, g_kv, w_kv, g_k, g_b, w_q, g_q, ones):
    m, d = x.shape
    d_ff = w_up.shape[1]
    nq = w_q.shape[1]
    nkv = w_kv.shape[1]
    nk = nq
    tm = ROW_TILE
    assert m % tm == 0 and d_ff % FF_CHUNK == 0

    def rows(n):
        return pl.BlockSpec((tm, n), lambda i: (i, 0))

    return pl.pallas_call(
        _mlp_qkv_kernel,
        grid=(m // tm,),
        in_specs=[rows(d), _const_spec((1, d)), _const_spec((d, d_ff)),
                  _const_spec((d_ff, d)), _const_spec((1, d)),
                  _const_spec((d, nkv)), _const_spec((1, nk)),
                  _const_spec((1, d)), _const_spec((d, nq)),
                  _const_spec((1, nq)),
                  _const_spec((V7X_MXU_DIM, V7X_MXU_DIM))],
        out_specs=[rows(d), rows(nq), rows(nk), rows(nkv - nk)],
        out_shape=[jax.ShapeDtypeStruct((m, d), _F32),
                   jax.ShapeDtypeStruct((m, nq), _BF16),
                   jax.ShapeDtypeStruct((m, nk), _BF16),
                   jax.ShapeDtypeStruct((m, nkv - nk), _BF16)],
        compiler_params=pltpu.CompilerParams(
            dimension_semantics=("arbitrary",),
            vmem_limit_bytes=V7X_VMEM_LIMIT_BYTES),
        name="mlp_qkv",
    )(x, g_mlp, w_up, w_down, g_kv, w_kv, g_k, g_b, w_q, g_q, ones)


def _attn_kernel(q_ref, k_ref, v_ref, lq1_ref, lk1_ref, lq2_ref, lk2_ref,
                 sub_ref, o_ref, *, lam_init):
    s_len = q_ref.shape[1]
    tq, tk = Q_TILE, K_TILE
    lam = (jnp.exp(jnp.sum(lq1_ref[...] * lk1_ref[...], keepdims=True))
           - jnp.exp(jnp.sum(lq2_ref[...] * lk2_ref[...], keepdims=True))
           + lam_init)
    out_gain = sub_ref[...] * (1.0 - lam_init)

    lane = lax.broadcasted_iota(jnp.int32, (tq, V_DIM), 1)
    row_id = lax.broadcasted_iota(jnp.int32, (tq, tk), 0)
    col_id = lax.broadcasted_iota(jnp.int32, (tq, tk), 1)

    def online_update(s, m, l, acc, vc):
        m_new = jnp.maximum(m, jnp.max(s, axis=-1, keepdims=True))
        alpha = jnp.exp(m - m_new)
        p = jnp.exp(s - m_new)
        l = alpha * l + jnp.sum(p, axis=-1, keepdims=True)
        acc = alpha * acc + _dot(p.astype(_BF16), vc)
        return m_new, l, acc

    for qi in range(s_len // tq):
        q_lo = qi * tq
        q = q_ref[0, q_lo:q_lo + tq, :]
        zero = jnp.zeros_like(q)
        q_maps = (jnp.where(lane < HEAD_DIM, q, zero),
                  jnp.where(lane >= HEAD_DIM, q, zero))

        def step(k_lo, carry, *, on_diagonal, q_maps=q_maps, q_lo=q_lo):
            kc = k_ref[0, pl.ds(k_lo, tk), :]
            vc = v_ref[0, pl.ds(k_lo, tk), :]
            new = []
            for c in range(2):
                s = _dot_nt(q_maps[c], kc)
                if on_diagonal:
                    visible = (k_lo + col_id) <= (q_lo + row_id)
                    s = jnp.where(visible, s, NEG_INF)
                new.extend(online_update(s, *carry[3 * c:3 * c + 3], vc))
            return tuple(new)

        init = []
        for c in range(2):
            init += [jnp.full((tq, 1), NEG_INF, _F32), jnp.zeros((tq, 1), _F32),
                     jnp.zeros((tq, V_DIM), _F32)]
        n_below = q_lo // tk
        carry = lax.fori_loop(
            0, n_below,
            lambda j, c: step(pl.multiple_of(j * tk, tk), c, on_diagonal=False),
            tuple(init))
        for k_lo in range(n_below * tk, q_lo + tq, tk):
            carry = step(k_lo, carry, on_diagonal=True)

        _, l0, a0, _, l1, a1 = carry
        o = a0 / l0 - lam * (a1 / l1)
        o_ref[0, q_lo:q_lo + tq, :] = (_rms(o, out_gain)).astype(_BF16)


def _attention(q, k, v, lq1, lk1, lq2, lk2, sub, lam_init):
    b, s, n = q.shape
    n_heads = n // V_DIM
    assert s % Q_TILE == 0 and s % K_TILE == 0
    head = pl.BlockSpec((1, s, V_DIM), lambda i, h: (i, 0, h))
    vec = _const_spec((1, HEAD_DIM))
    return pl.pallas_call(
        functools.partial(_attn_kernel, lam_init=lam_init),
        grid=(b, n_heads),
        in_specs=[head, head, head, vec, vec, vec, vec, _const_spec((1, V_DIM))],
        out_specs=head,
        out_shape=jax.ShapeDtypeStruct((b, s, n), _BF16),
        compiler_params=pltpu.CompilerParams(
            dimension_semantics=("arbitrary", "arbitrary"),
            vmem_limit_bytes=V7X_VMEM_LIMIT_BYTES),
        name="attention",
    )(q, k, v, lq1, lk1, lq2, lk2, sub)


def _out_mlp_kernel(x_ref, o_ref, wo_ref, gm_ref, wup_ref, wdn_ref, y_ref):
    x2 = x_ref[...] + _dot(o_ref[...], wo_ref[...])
    y_ref[...] = _mlp(x2, gm_ref[...], wup_ref, wdn_ref)


def _out_mlp(x, o, w_o, g_mlp, w_up, w_down):
    m, d = x.shape
    n = o.shape[1]
    d_ff = w_up.shape[1]
    tm = ROW_TILE
    assert m % tm == 0 and d_ff % FF_CHUNK == 0
    return pl.pallas_call(
        _out_mlp_kernel,
        grid=(m // tm,),
        in_specs=[pl.BlockSpec((tm, d), lambda i: (i, 0)),
                  pl.BlockSpec((tm, n), lambda i: (i, 0)),
                  _const_spec((n, d)), _const_spec((1, d)),
                  _const_spec((d, d_ff)), _const_spec((d_ff, d))],
        out_specs=pl.BlockSpec((tm, d), lambda i: (i, 0)),
        out_shape=jax.ShapeDtypeStruct((m, d), _F32),
        compiler_params=pltpu.CompilerParams(
            dimension_semantics=("arbitrary",),
            vmem_limit_bytes=V7X_VMEM_LIMIT_BYTES),
        name="out_mlp",
    )(x, o, w_o, g_mlp, w_up, w_down)


def _lambda_init(layer_idx):
    return 0.8 - 0.6 * math.exp(-0.3 * layer_idx)


def kernel(x, a_norm, a_w_in, a_conv, a_w_out, kv_norm, w_kv, k_norm, b_norm,
           w_q, q_norm, lam_q1, lam_k1, lam_q2, lam_k2, sub_norm, w_o,
           mlp_norm, w_up, w_down):
    b, s, d = x.shape
    n_a, n_b = a_norm.shape[0], b_norm.shape[0]
    assert n_a == 1 and n_b == 1, "one mixer layer followed by one attention layer"
    n_qk = w_q.shape[2]
    assert w_kv.shape[1] == 2 * n_qk and n_qk % V7X_MXU_DIM == 0

    bf = lambda w: w.astype(_BF16)
    row = lambda g: g.reshape(1, -1)
    groups = n_qk // HEAD_DIM
    gid = jnp.arange(V7X_MXU_DIM) // HEAD_DIM
    ones = (gid[:, None] == gid[None, :]).astype(_BF16)

    x = _mixer(x, row(a_norm[0]), bf(a_w_in[0]), a_conv[0], bf(a_w_out[0]))
    x1, q, k, v = _mlp_qkv(
        x.reshape(b * s, d), row(mlp_norm[0]), bf(w_up[0]), bf(w_down[0]),
        row(kv_norm), bf(w_kv), row(jnp.tile(k_norm, groups)),
        row(b_norm[0]), bf(w_q[0]), row(jnp.tile(q_norm[0], groups)), ones)
    o = _attention(q.reshape(b, s, n_qk), k.reshape(b, s, n_qk),
                   v.reshape(b, s, -1), row(lam_q1[0]), row(lam_k1[0]),
                   row(lam_q2[0]), row(lam_k2[0]), row(sub_norm[0]),
                   _lambda_init(n_a))
    y = _out_mlp(x1, o.reshape(b * s, -1), bf(w_o[0]), row(mlp_norm[1]),
                 bf(w_up[1]), bf(w_down[1]))
    return y.reshape(b, s, d)
```

```python
import functools
import math

import jax
import jax.numpy as jnp
from jax import lax
from jax.experimental import pallas as pl
from jax.experimental.pallas import tpu as pltpu

EPS = 1e-6
NEG_INF = -1e30
HEAD_DIM = 64
V_DIM = 2 * HEAD_DIM
CONV_WIDTH = 3

V7X_LANES = 128
V7X_SUBLANES = 8
V7X_MXU_DIM = 256
V7X_VMEM_LIMIT_BYTES = 60 * 1024 * 1024

ROW_TILE = 512
FF_CHUNK = 1024
Q_TILE = 256
K_TILE = 256

_LOG2_E = math.log2(math.e)
_BF16 = jnp.bfloat16
_F32 = jnp.float32


def _dot(a, b):
    return jnp.dot(a, b, preferred_element_type=_F32)


def _dot_nt(a, b):
    return lax.dot_general(a, b, (((1,), (1,)), ((), ())),
                           preferred_element_type=_F32)


def _rms(x, g):
    ms = jnp.mean(x * x, axis=-1, keepdims=True)
    return x * lax.rsqrt(ms + EPS) * g


def _mlp(x, g, wup_ref, wdn_ref):
    h = _rms(x, g).astype(_BF16)
    acc = x
    d_ff = wup_ref.shape[1]
    for c in range(d_ff // FF_CHUNK):
        sl = slice(c * FF_CHUNK, (c + 1) * FF_CHUNK)
        a = jnp.maximum(_dot(h, wup_ref[:, sl]), 0.0)
        acc = acc + _dot((a * a).astype(_BF16), wdn_ref[sl, :])
    return acc


def _head_rms(y, ones_ref, gain):
    y2 = (y * y).astype(_BF16)
    n = y.shape[1]
    parts = [_dot(y2[:, t:t + V7X_MXU_DIM], ones_ref[...])
             for t in range(0, n, V7X_MXU_DIM)]
    ss = jnp.concatenate(parts, axis=1)
    return y * lax.rsqrt(ss * (1.0 / HEAD_DIM) + EPS) * gain


def _mixer_kernel(x_ref, g_ref, win_ref, conv_ref, wout_ref, o_ref, carry_ref):
    d = x_ref.shape[2]
    ts = x_ref.shape[1]

    @pl.when(pl.program_id(1) == 0)
    def _():
        carry_ref[...] = jnp.zeros_like(carry_ref)

    x = x_ref[0]
    h = _rms(x, g_ref[...]).astype(_BF16)
    b_gate = _dot(h, win_ref[:, 0:d])
    c_gate = _dot(h, win_ref[:, d:2 * d])
    v = _dot(h, win_ref[:, 2 * d:3 * d])
    u = c_gate * v

    rows = lax.broadcasted_iota(jnp.int32, u.shape, 0)
    tail = carry_ref[...]
    prev1 = tail[V7X_SUBLANES - 1:V7X_SUBLANES]
    prev2 = tail[V7X_SUBLANES - 2:V7X_SUBLANES - 1]
    u1 = jnp.where(rows == 0, prev1, pltpu.roll(u, 1, 0))
    u2 = jnp.where(rows == 0, prev2,
                   jnp.where(rows == 1, prev1, pltpu.roll(u, 2, 0)))
    w = conv_ref[...]
    y = w[0:1] * u2 + w[1:2] * u1 + w[2:3] * u
    carry_ref[...] = u[ts - V7X_SUBLANES:ts]

    o_ref[0] = x + _dot((b_gate * y).astype(_BF16), wout_ref[...])


def _const_spec(shape):
    return pl.BlockSpec(shape, lambda *_: (0,) * len(shape),
                        pipeline_mode=pl.Buffered(1))


def _mixer(x, g, w_in, conv_w, w_out):
    b, s, d = x.shape
    ts = ROW_TILE
    assert s % ts == 0
    tok = pl.BlockSpec((1, ts, d), lambda i, j: (i, j, 0))
    return pl.pallas_call(
        _mixer_kernel,
        grid=(b, s // ts),
        in_specs=[tok, _const_spec((1, d)), _const_spec((d, 3 * d)),
                  _const_spec((CONV_WIDTH, d)), _const_spec((d, d))],
        out_specs=tok,
        out_shape=jax.ShapeDtypeStruct(x.shape, _F32),
        scratch_shapes=[pltpu.VMEM((V7X_SUBLANES, d), _F32)],
        compiler_params=pltpu.CompilerParams(
            dimension_semantics=("arbitrary", "arbitrary"),
            vmem_limit_bytes=V7X_VMEM_LIMIT_BYTES),
        name="mixer",
    )(x, g, w_in, conv_w, w_out)


def _mlp_qkv_kernel(x_ref, gm_ref, wup_ref, wdn_ref, gkv_ref, wkv_ref, gk_ref,
                    gb_ref, wq_ref, gq_ref, ones_ref,
                    x1_ref, q_ref, k_ref, v_ref):
    x1 = _mlp(x_ref[...], gm_ref[...], wup_ref, wdn_ref)
    x1_ref[...] = x1

    nk = k_ref.shape[1]
    hkv = _rms(x1, gkv_ref[...]).astype(_BF16)
    k = _dot(hkv, wkv_ref[:, 0:nk])
    k_ref[...] = _head_rms(k, ones_ref, gk_ref[...]).astype(_BF16)
    v_ref[...] = _dot(hkv, wkv_ref[:, nk:]).astype(_BF16)

    hq = _rms(x1, gb_ref[...]).astype(_BF16)
    q = _head_rms(_dot(hq, wq_ref[...]), ones_ref, gq_ref[...])
    q_ref[...] = (q * (HEAD_DIM ** -0.5 * _LOG2_E)).astype(_BF16)


def _mlp_qkv(x, g_mlp, w_up, w_down, g_kv, w_kv, g_k, g_b, w_q, g_q, ones):
    m, d = x.shape
    d_ff = w_up.shape[1]
    nq = w_q.shape[1]
    nkv = w_kv.shape[1]
    nk = nq
    tm = ROW_TILE
    assert m % tm == 0 and d_ff % FF_CHUNK == 0

    def rows(n):
        return pl.BlockSpec((tm, n), lambda i: (i, 0))

    return pl.pallas_call(
        _mlp_qkv_kernel,
        grid=(m // tm,),
        in_specs=[rows(d), _const_spec((1, d)), _const_spec((d, d_ff)),
                  _const_spec((d_ff, d)), _const_spec((1, d)),
                  _const_spec((d, nkv)), _const_spec((1, nk)),
                  _const_spec((1, d)), _const_spec((d, nq)),
                  _const_spec((1, nq)),
                  _const_spec((V7X_MXU_DIM, V7X_MXU_DIM))],
        out_specs=[rows(d), rows(nq), rows(nk), rows(nkv - nk)],
        out_shape=[jax.ShapeDtypeStruct((m, d), _F32),
                   jax.ShapeDtypeStruct((m, nq), _BF16),
                   jax.ShapeDtypeStruct((m, nk), _BF16),
                   jax.ShapeDtypeStruct((m, nkv - nk), _BF16)],
        compiler_params=pltpu.CompilerParams(
            dimension_semantics=("arbitrary",),
            vmem_limit_bytes=V7X_VMEM_LIMIT_BYTES),
        name="mlp_qkv",
    )(x, g_mlp, w_up, w_down, g_kv, w_kv, g_k, g_b, w_q, g_q, ones)


def _attn_kernel(q_ref, k_ref, v_ref, lq1_ref, lk1_ref, lq2_ref, lk2_ref,
                 sub_ref, o_ref, s_scr, p_scr, *, lam_init):
    s_len = q_ref.shape[1]
    tq, tk = Q_TILE, K_TILE
    lam = (jnp.exp(jnp.sum(lq1_ref[...] * lk1_ref[...], keepdims=True))
           - jnp.exp(jnp.sum(lq2_ref[...] * lk2_ref[...], keepdims=True))
           + lam_init)
    out_gain = sub_ref[...] * (1.0 - lam_init)

    lane = lax.broadcasted_iota(jnp.int32, (tq, V_DIM), 1)
    row_id = lax.broadcasted_iota(jnp.int32, (tq, tk), 0)
    col_id = lax.broadcasted_iota(jnp.int32, (tq, tk), 1)

    def lane_tiles(a):
        return [a[:, t:t + V7X_LANES] for t in range(0, a.shape[1], V7X_LANES)]

    for qi in range(s_len // tq):
        q_lo = qi * tq
        n_keys = q_lo + tq
        q = q_ref[0, q_lo:q_lo + tq, :]
        zero = jnp.zeros_like(q)
        q_maps = (jnp.where(lane < HEAD_DIM, q, zero),
                  jnp.where(lane >= HEAD_DIM, q, zero))
        heads = []
        for c in range(2):
            m_run = jnp.full((tq, V7X_LANES), NEG_INF, _F32)
            for k_lo in range(0, n_keys, tk):
                s = _dot_nt(q_maps[c], k_ref[0, k_lo:k_lo + tk, :])
                if k_lo + tk > q_lo:
                    visible = (k_lo + col_id) <= (q_lo + row_id)
                    s = jnp.where(visible, s, NEG_INF)
                s_scr[c, :, k_lo:k_lo + tk] = s
                for part in lane_tiles(s):
                    m_run = jnp.maximum(m_run, part)
            m = jnp.max(m_run, axis=-1, keepdims=True)
            l_run = jnp.zeros((tq, V7X_LANES), _F32)
            for k_lo in range(0, n_keys, tk):
                p = jnp.exp2(s_scr[c, :, k_lo:k_lo + tk] - m)
                p_scr[c, :, k_lo:k_lo + tk] = p.astype(_BF16)
                for part in lane_tiles(p):
                    l_run = l_run + part
            l = jnp.sum(l_run, axis=-1, keepdims=True)
            acc = _dot(p_scr[c, :, 0:n_keys], v_ref[0, 0:n_keys, :])
            heads.append(acc / l)
        o = heads[0] - lam * heads[1]
        o_ref[0, q_lo:q_lo + tq, :] = (_rms(o, out_gain)).astype(_BF16)


def _attention(q, k, v, lq1, lk1, lq2, lk2, sub, lam_init):
    b, s, n = q.shape
    n_heads = n // V_DIM
    assert s % Q_TILE == 0 and s % K_TILE == 0
    head = pl.BlockSpec((1, s, V_DIM), lambda i, h: (i, 0, h))
    vec = _const_spec((1, HEAD_DIM))
    return pl.pallas_call(
        functools.partial(_attn_kernel, lam_init=lam_init),
        grid=(b, n_heads),
        in_specs=[head, head, head, vec, vec, vec, vec, _const_spec((1, V_DIM))],
        out_specs=head,
        out_shape=jax.ShapeDtypeStruct((b, s, n), _BF16),
        scratch_shapes=[pltpu.VMEM((2, Q_TILE, s), _F32),
                        pltpu.VMEM((2, Q_TILE, s), _BF16)],
        compiler_params=pltpu.CompilerParams(
            dimension_semantics=("arbitrary", "arbitrary"),
            vmem_limit_bytes=V7X_VMEM_LIMIT_BYTES),
        name="attention",
    )(q, k, v, lq1, lk1, lq2, lk2, sub)


def _out_mlp_kernel(x_ref, o_ref, wo_ref, gm_ref, wup_ref, wdn_ref, y_ref):
    x2 = x_ref[...] + _dot(o_ref[...], wo_ref[...])
    y_ref[...] = _mlp(x2, gm_ref[...], wup_ref, wdn_ref)


def _out_mlp(x, o, w_o, g_mlp, w_up, w_down):
    m, d = x.shape
    n = o.shape[1]
    d_ff = w_up.shape[1]
    tm = ROW_TILE
    assert m % tm == 0 and d_ff % FF_CHUNK == 0
    return pl.pallas_call(
        _out_mlp_kernel,
        grid=(m // tm,),
        in_specs=[pl.BlockSpec((tm, d), lambda i: (i, 0)),
                  pl.BlockSpec((tm, n), lambda i: (i, 0)),
                  _const_spec((n, d)), _const_spec((1, d)),
                  _const_spec((d, d_ff)), _const_spec((d_ff, d))],
        out_specs=pl.BlockSpec((tm, d), lambda i: (i, 0)),
        out_shape=jax.ShapeDtypeStruct((m, d), _F32),
        compiler_params=pltpu.CompilerParams(
            dimension_semantics=("arbitrary",),
            vmem_limit_bytes=V7X_VMEM_LIMIT_BYTES),
        name="out_mlp",
    )(x, o, w_o, g_mlp, w_up, w_down)


def _lambda_init(layer_idx):
    return 0.8 - 0.6 * math.exp(-0.3 * layer_idx)


def kernel(x, a_norm, a_w_in, a_conv, a_w_out, kv_norm, w_kv, k_norm, b_norm,
           w_q, q_norm, lam_q1, lam_k1, lam_q2, lam_k2, sub_norm, w_o,
           mlp_norm, w_up, w_down):
    b, s, d = x.shape
    n_a, n_b = a_norm.shape[0], b_norm.shape[0]
    assert n_a == 1 and n_b == 1, "one mixer layer followed by one attention layer"
    n_qk = w_q.shape[2]
    assert w_kv.shape[1] == 2 * n_qk and n_qk % V7X_MXU_DIM == 0

    bf = lambda w: w.astype(_BF16)
    row = lambda g: g.reshape(1, -1)
    groups = n_qk // HEAD_DIM
    gid = jnp.arange(V7X_MXU_DIM) // HEAD_DIM
    ones = (gid[:, None] == gid[None, :]).astype(_BF16)

    x = _mixer(x, row(a_norm[0]), bf(a_w_in[0]), a_conv[0], bf(a_w_out[0]))
    x1, q, k, v = _mlp_qkv(
        x.reshape(b * s, d), row(mlp_norm[0]), bf(w_up[0]), bf(w_down[0]),
        row(kv_norm), bf(w_kv), row(jnp.tile(k_norm, groups)),
        row(b_norm[0]), bf(w_q[0]), row(jnp.tile(q_norm[0], groups)), ones)
    o = _attention(q.reshape(b, s, n_qk), k.reshape(b, s, n_qk),
                   v.reshape(b, s, -1), row(lam_q1[0]), row(lam_k1[0]),
                   row(lam_q2[0]), row(lam_k2[0]), row(sub_norm[0]),
                   _lambda_init(n_a))
    y = _out_mlp(x1, o.reshape(b * s, -1), bf(w_o[0]), row(mlp_norm[1]),
                 bf(w_up[1]), bf(w_down[1]))
    return y.reshape(b, s, d)
```

```python
import functools
import math

import jax
import jax.numpy as jnp
from jax import lax
from jax.experimental import pallas as pl
from jax.experimental.pallas import tpu as pltpu

EPS = 1e-6
NEG_INF = -1e30
HEAD_DIM = 64
V_DIM = 2 * HEAD_DIM
CONV_WIDTH = 3

V7X_LANES = 128
V7X_SUBLANES = 8
V7X_MXU_DIM = 256
V7X_VMEM_LIMIT_BYTES = 60 * 1024 * 1024

ROW_TILE = 512
FF_CHUNK = 1024
Q_TILE = 256
K_TILE = 256

_LOG2_E = math.log2(math.e)
_BF16 = jnp.bfloat16
_F32 = jnp.float32


def _dot(a, b):
    return jnp.dot(a, b, preferred_element_type=_F32)


def _dot_nt(a, b):
    return lax.dot_general(a, b, (((1,), (1,)), ((), ())),
                           preferred_element_type=_F32)


def _rms(x, g):
    ms = jnp.mean(x * x, axis=-1, keepdims=True)
    return x * lax.rsqrt(ms + EPS) * g


def _mlp(x, g, wup_ref, wdn_ref):
    h = _rms(x, g).astype(_BF16)
    acc = x
    d_ff = wup_ref.shape[1]
    for c in range(d_ff // FF_CHUNK):
        sl = slice(c * FF_CHUNK, (c + 1) * FF_CHUNK)
        a = jnp.maximum(_dot(h, wup_ref[:, sl]), 0.0)
        acc = acc + _dot((a * a).astype(_BF16), wdn_ref[sl, :])
    return acc


def _head_rms(y, ones_ref, gain):
    y2 = (y * y).astype(_BF16)
    n = y.shape[1]
    parts = [_dot(y2[:, t:t + V7X_MXU_DIM], ones_ref[...])
             for t in range(0, n, V7X_MXU_DIM)]
    ss = jnp.concatenate(parts, axis=1)
    return y * lax.rsqrt(ss * (1.0 / HEAD_DIM) + EPS) * gain


def _mixer_kernel(x_ref, g_ref, win_ref, conv_ref, wout_ref, o_ref, carry_ref):
    d = x_ref.shape[2]
    ts = x_ref.shape[1]

    @pl.when(pl.program_id(1) == 0)
    def _():
        carry_ref[...] = jnp.zeros_like(carry_ref)

    x = x_ref[0]
    h = _rms(x, g_ref[...]).astype(_BF16)
    b_gate = _dot(h, win_ref[:, 0:d])
    c_gate = _dot(h, win_ref[:, d:2 * d])
    v = _dot(h, win_ref[:, 2 * d:3 * d])
    u = c_gate * v

    rows = lax.broadcasted_iota(jnp.int32, u.shape, 0)
    tail = carry_ref[...]
    prev1 = tail[V7X_SUBLANES - 1:V7X_SUBLANES]
    prev2 = tail[V7X_SUBLANES - 2:V7X_SUBLANES - 1]
    u1 = jnp.where(rows == 0, prev1, pltpu.roll(u, 1, 0))
    u2 = jnp.where(rows == 0, prev2,
                   jnp.where(rows == 1, prev1, pltpu.roll(u, 2, 0)))
    w = conv_ref[...]
    y = w[0:1] * u2 + w[1:2] * u1 + w[2:3] * u
    carry_ref[...] = u[ts - V7X_SUBLANES:ts]

    o_ref[0] = x + _dot((b_gate * y).astype(_BF16), wout_ref[...])


def _const_spec(shape):
    return pl.BlockSpec(shape, lambda *_: (0,) * len(shape),
                        pipeline_mode=pl.Buffered(1))


def _mixer(x, g, w_in, conv_w, w_out):
    b, s, d = x.shape
    ts = ROW_TILE
    assert s % ts == 0
    tok = pl.BlockSpec((1, ts, d), lambda i, j: (i, j, 0))
    return pl.pallas_call(
        _mixer_kernel,
        grid=(b, s // ts),
        in_specs=[tok, _const_spec((1, d)), _const_spec((d, 3 * d)),
                  _const_spec((CONV_WIDTH, d)), _const_spec((d, d))],
        out_specs=tok,
        out_shape=jax.ShapeDtypeStruct(x.shape, _F32),
        scratch_shapes=[pltpu.VMEM((V7X_SUBLANES, d), _F32)],
        compiler_params=pltpu.CompilerParams(
            dimension_semantics=("arbitrary", "arbitrary"),
            vmem_limit_bytes=V7X_VMEM_LIMIT_BYTES),
        name="mixer",
    )(x, g, w_in, conv_w, w_out)


def _mlp_qkv_kernel(x_ref, gm_ref, wup_ref, wdn_ref, gkv_ref, wkv_ref, gk_ref,
                    gb_ref, wq_ref, gq_ref, ones_ref,
                    x1_ref, q_ref, k_ref, v_ref):
    x1 = _mlp(x_ref[...], gm_ref[...], wup_ref, wdn_ref)
    x1_ref[...] = x1

    nk = k_ref.shape[1]
    hkv = _rms(x1, gkv_ref[...]).astype(_BF16)
    k = _dot(hkv, wkv_ref[:, 0:nk])
    k_ref[...] = _head_rms(k, ones_ref, gk_ref[...]).astype(_BF16)
    v_ref[...] = _dot(hkv, wkv_ref[:, nk:]).astype(_BF16)

    hq = _rms(x1, gb_ref[...]).astype(_BF16)
    q = _head_rms(_dot(hq, wq_ref[...]), ones_ref, gq_ref[...])
    q_ref[...] = (q * (HEAD_DIM ** -0.5 * _LOG2_E)).astype(_BF16)


def _mlp_qkv(x, g_mlp, w_up, w_down, g_kv, w_kv, g_k, g_b, w_q, g_q, ones):
    m, d = x.shape
    d_ff = w_up.shape[1]
    nq = w_q.shape[1]
    nkv = w_kv.shape[1]
    nk = nq
    tm = ROW_TILE
    assert m % tm == 0 and d_ff % FF_CHUNK == 0

    def rows(n):
        return pl.BlockSpec((tm, n), lambda i: (i, 0))

    return pl.pallas_call(
        _mlp_qkv_kernel,
        grid=(m // tm,),
        in_specs=[rows(d), _const_spec((1, d)), _const_spec((d, d_ff)),
                  _const_spec((d_ff, d)), _const_spec((1, d)),
                  _const_spec((d, nkv)), _const_spec((1, nk)),
                  _const_spec((1, d)), _const_spec((d, nq)),
                  _const_spec((1, nq)),
                  _const_spec((V7X_MXU_DIM, V7X_MXU_DIM))],
        out_specs=[rows(d), rows(nq), rows(nk), rows(nkv - nk)],
        out_shape=[jax.ShapeDtypeStruct((m, d), _F32),
                   jax.ShapeDtypeStruct((m, nq), _BF16),
                   jax.ShapeDtypeStruct((m, nk), _BF16),
                   jax.ShapeDtypeStruct((m, nkv - nk), _BF16)],
        compiler_params=pltpu.CompilerParams(
            dimension_semantics=("arbitrary",),
            vmem_limit_bytes=V7X_VMEM_LIMIT_BYTES),
        name="mlp_qkv",
    )(x, g_mlp, w_up, w_down, g_kv, w_kv, g_k, g_b, w_q, g_q, ones)


def _attn_kernel(q_ref, k_ref, v_ref, lq1_ref, lk1_ref, lq2_ref, lk2_ref,
                 sub_ref, o_ref, s_scr, p_scr, vx_scr, *, lam_init):
    s_len = q_ref.shape[1]
    tq, tk = Q_TILE, K_TILE
    lam = (jnp.exp(jnp.sum(lq1_ref[...] * lk1_ref[...], keepdims=True))
           - jnp.exp(jnp.sum(lq2_ref[...] * lk2_ref[...], keepdims=True))
           + lam_init)
    out_gain = sub_ref[...] * (1.0 - lam_init)

    vx_scr[:, 0:V_DIM] = v_ref[0]
    vx_scr[:, V_DIM:2 * V_DIM] = jnp.ones((s_len, V_DIM), _BF16)

    lane = lax.broadcasted_iota(jnp.int32, (tq, V_DIM), 1)
    row_id = lax.broadcasted_iota(jnp.int32, (tq, tk), 0)
    row_id = jnp.concatenate([row_id, row_id], axis=0)
    col_id = lax.broadcasted_iota(jnp.int32, (2 * tq, tk), 1)

    def lane_tiles(a):
        return [a[:, t:t + V7X_LANES] for t in range(0, a.shape[1], V7X_LANES)]

    for qi in reversed(range(s_len // tq)):
        slot = qi % 2
        q_lo = qi * tq
        n_keys = q_lo + tq
        q = q_ref[0, q_lo:q_lo + tq, :]
        zero = jnp.zeros_like(q)
        q2 = jnp.concatenate([jnp.where(lane < HEAD_DIM, q, zero),
                              jnp.where(lane >= HEAD_DIM, q, zero)], axis=0)
        m_run = jnp.full((2 * tq, V7X_LANES), NEG_INF, _F32)
        for k_lo in range(0, n_keys, tk):
            s = _dot_nt(q2, k_ref[0, k_lo:k_lo + tk, :])
            if k_lo + tk > q_lo:
                visible = (k_lo + col_id) <= (q_lo + row_id)
                s = jnp.where(visible, s, NEG_INF)
            s_scr[slot, :, k_lo:k_lo + tk] = s
            for part in lane_tiles(s):
                m_run = jnp.maximum(m_run, part)
        m = jnp.max(m_run, axis=-1, keepdims=True)
        for k_lo in range(0, n_keys, tk):
            p = jnp.exp2(s_scr[slot, :, k_lo:k_lo + tk] - m)
            p_scr[slot, :, k_lo:k_lo + tk] = p.astype(_BF16)
        pv = _dot(p_scr[slot, :, 0:n_keys], vx_scr[0:n_keys, :])
        heads = pv[:, 0:V_DIM] / pv[:, V_DIM:2 * V_DIM]
        o = heads[0:tq] - lam * heads[tq:2 * tq]
        o_ref[0, q_lo:q_lo + tq, :] = (_rms(o, out_gain)).astype(_BF16)


def _attention(q, k, v, lq1, lk1, lq2, lk2, sub, lam_init):
    b, s, n = q.shape
    n_heads = n // V_DIM
    assert s % Q_TILE == 0 and s % K_TILE == 0
    head = pl.BlockSpec((1, s, V_DIM), lambda i, h: (i, 0, h))
    vec = _const_spec((1, HEAD_DIM))
    return pl.pallas_call(
        functools.partial(_attn_kernel, lam_init=lam_init),
        grid=(b, n_heads),
        in_specs=[head, head, head, vec, vec, vec, vec, _const_spec((1, V_DIM))],
        out_specs=head,
        out_shape=jax.ShapeDtypeStruct((b, s, n), _BF16),
        scratch_shapes=[pltpu.VMEM((2, 2 * Q_TILE, s), _F32),
                        pltpu.VMEM((2, 2 * Q_TILE, s), _BF16),
                        pltpu.VMEM((s, 2 * V_DIM), _BF16)],
        compiler_params=pltpu.CompilerParams(
            dimension_semantics=("arbitrary", "arbitrary"),
            vmem_limit_bytes=V7X_VMEM_LIMIT_BYTES),
        name="attention",
    )(q, k, v, lq1, lk1, lq2, lk2, sub)


def _out_mlp_kernel(x_ref, o_ref, wo_ref, gm_ref, wup_ref, wdn_ref, y_ref):
    x2 = x_ref[...] + _dot(o_ref[...], wo_ref[...])
    y_ref[...] = _mlp(x2, gm_ref[...], wup_ref, wdn_ref)


def _out_mlp(x, o, w_o, g_mlp, w_up, w_down):
    m, d = x.shape
    n = o.shape[1]
    d_ff = w_up.shape[1]
    tm = ROW_TILE
    assert m % tm == 0 and d_ff % FF_CHUNK == 0
    return pl.pallas_call(
        _out_mlp_kernel,
        grid=(m // tm,),
        in_specs=[pl.BlockSpec((tm, d), lambda i: (i, 0)),
                  pl.BlockSpec((tm, n), lambda i: (i, 0)),
                  _const_spec((n, d)), _const_spec((1, d)),
                  _const_spec((d, d_ff)), _const_spec((d_ff, d))],
        out_specs=pl.BlockSpec((tm, d), lambda i: (i, 0)),
        out_shape=jax.ShapeDtypeStruct((m, d), _F32),
        compiler_params=pltpu.CompilerParams(
            dimension_semantics=("arbitrary",),
            vmem_limit_bytes=V7X_VMEM_LIMIT_BYTES),
        name="out_mlp",
    )(x, o, w_o, g_mlp, w_up, w_down)


def _lambda_init(layer_idx):
    return 0.8 - 0.6 * math.exp(-0.3 * layer_idx)


def kernel(x, a_norm, a_w_in, a_conv, a_w_out, kv_norm, w_kv, k_norm, b_norm,
           w_q, q_norm, lam_q1, lam_k1, lam_q2, lam_k2, sub_norm, w_o,
           mlp_norm, w_up, w_down):
    b, s, d = x.shape
    n_a, n_b = a_norm.shape[0], b_norm.shape[0]
    assert n_a == 1 and n_b == 1, "one mixer layer followed by one attention layer"
    n_qk = w_q.shape[2]
    assert w_kv.shape[1] == 2 * n_qk and n_qk % V7X_MXU_DIM == 0

    bf = lambda w: w.astype(_BF16)
    row = lambda g: g.reshape(1, -1)
    groups = n_qk // HEAD_DIM
    gid = jnp.arange(V7X_MXU_DIM) // HEAD_DIM
    ones = (gid[:, None] == gid[None, :]).astype(_BF16)

    x = _mixer(x, row(a_norm[0]), bf(a_w_in[0]), a_conv[0], bf(a_w_out[0]))
    x1, q, k, v = _mlp_qkv(
        x.reshape(b * s, d), row(mlp_norm[0]), bf(w_up[0]), bf(w_down[0]),
        row(kv_norm), bf(w_kv), row(jnp.tile(k_norm, groups)),
        row(b_norm[0]), bf(w_q[0]), row(jnp.tile(q_norm[0], groups)), ones)
    o = _attention(q.reshape(b, s, n_qk), k.reshape(b, s, n_qk),
                   v.reshape(b, s, -1), row(lam_q1[0]), row(lam_k1[0]),
                   row(lam_q2[0]), row(lam_k2[0]), row(sub_norm[0]),
                   _lambda_init(n_a))
    y = _out_mlp(x1, o.reshape(b * s, -1), bf(w_o[0]), row(mlp_norm[1]),
                 bf(w_up[1]), bf(w_down[1]))
    return y.reshape(b, s, d)
```

```python
import functools
import math

import jax
import jax.numpy as jnp
from jax import lax
from jax.experimental import pallas as pl
from jax.experimental.pallas import tpu as pltpu

EPS = 1e-6
NEG_INF = -1e30
HEAD_DIM = 64
V_DIM = 2 * HEAD_DIM
CONV_WIDTH = 3

V7X_LANES = 128
V7X_SUBLANES = 8
V7X_MXU_DIM = 256
V7X_BF16_ROWS_PER_VREG = 16
V7X_VMEM_LIMIT_BYTES = 60 * 1024 * 1024

ROW_TILE = 512
FF_CHUNK = 1024
Q_TILE = 256
K_TILE = 256
ONES_ROWS = V7X_BF16_ROWS_PER_VREG
SCORE_SLOTS = 2

_LOG2_E = math.log2(math.e)
_BF16 = jnp.bfloat16
_F32 = jnp.float32


def _dot(a, b):
    return jnp.dot(a, b, preferred_element_type=_F32)


def _dot_nt(a, b):
    return lax.dot_general(a, b, (((1,), (1,)), ((), ())),
                           preferred_element_type=_F32)


def _rms(x, g):
    ms = jnp.mean(x * x, axis=-1, keepdims=True)
    return x * lax.rsqrt(ms + EPS) * g


def _mlp(x, g, wup_ref, wdn_ref):
    h = _rms(x, g).astype(_BF16)
    acc = x
    d_ff = wup_ref.shape[1]
    for c in range(d_ff // FF_CHUNK):
        sl = slice(c * FF_CHUNK, (c + 1) * FF_CHUNK)
        a = jnp.maximum(_dot(h, wup_ref[:, sl]), 0.0)
        acc = acc + _dot((a * a).astype(_BF16), wdn_ref[sl, :])
    return acc


def _head_rms(y, ones_ref, gain):
    y2 = (y * y).astype(_BF16)
    n = y.shape[1]
    parts = [_dot(y2[:, t:t + V7X_MXU_DIM], ones_ref[...])
             for t in range(0, n, V7X_MXU_DIM)]
    ss = jnp.concatenate(parts, axis=1)
    return y * lax.rsqrt(ss * (1.0 / HEAD_DIM) + EPS) * gain


def _mixer_kernel(x_ref, g_ref, win_ref, conv_ref, wout_ref, o_ref, carry_ref):
    d = x_ref.shape[2]
    ts = x_ref.shape[1]

    @pl.when(pl.program_id(1) == 0)
    def _():
        carry_ref[...] = jnp.zeros_like(carry_ref)

    x = x_ref[0]
    h = _rms(x, g_ref[...]).astype(_BF16)
    b_gate = _dot(h, win_ref[:, 0:d])
    c_gate = _dot(h, win_ref[:, d:2 * d])
    v = _dot(h, win_ref[:, 2 * d:3 * d])
    u = c_gate * v

    rows = lax.broadcasted_iota(jnp.int32, u.shape, 0)
    tail = carry_ref[...]
    prev1 = tail[V7X_SUBLANES - 1:V7X_SUBLANES]
    prev2 = tail[V7X_SUBLANES - 2:V7X_SUBLANES - 1]
    u1 = jnp.where(rows == 0, prev1, pltpu.roll(u, 1, 0))
    u2 = jnp.where(rows == 0, prev2,
                   jnp.where(rows == 1, prev1, pltpu.roll(u, 2, 0)))
    w = conv_ref[...]
    y = w[0:1] * u2 + w[1:2] * u1 + w[2:3] * u
    carry_ref[...] = u[ts - V7X_SUBLANES:ts]

    o_ref[0] = x + _dot((b_gate * y).astype(_BF16), wout_ref[...])


def _const_spec(shape):
    return pl.BlockSpec(shape, lambda *_: (0,) * len(shape),
                        pipeline_mode=pl.Buffered(1))


def _mixer(x, g, w_in, conv_w, w_out):
    b, s, d = x.shape
    ts = ROW_TILE
    assert s % ts == 0
    tok = pl.BlockSpec((1, ts, d), lambda i, j: (i, j, 0))
    return pl.pallas_call(
        _mixer_kernel,
        grid=(b, s // ts),
        in_specs=[tok, _const_spec((1, d)), _const_spec((d, 3 * d)),
                  _const_spec((CONV_WIDTH, d)), _const_spec((d, d))],
        out_specs=tok,
        out_shape=jax.ShapeDtypeStruct(x.shape, _F32),
        scratch_shapes=[pltpu.VMEM((V7X_SUBLANES, d), _F32)],
        compiler_params=pltpu.CompilerParams(
            dimension_semantics=("arbitrary", "arbitrary"),
            vmem_limit_bytes=V7X_VMEM_LIMIT_BYTES),
        name="mixer",
    )(x, g, w_in, conv_w, w_out)


def _mlp_qkv_kernel(x_ref, gm_ref, wup_ref, wdn_ref, gkv_ref, wk_ref, wvt_ref,
                    gk_ref, gb_ref, wq_ref, gq_ref, ones_ref,
                    x1_ref, q_ref, k_ref, vt_ref):
    x1 = _mlp(x_ref[...], gm_ref[...], wup_ref, wdn_ref)
    x1_ref[...] = x1

    hkv = _rms(x1, gkv_ref[...]).astype(_BF16)
    k = _dot(hkv, wk_ref[...])
    k_ref[...] = _head_rms(k, ones_ref, gk_ref[...]).astype(_BF16)
    vt_ref[0] = _dot_nt(wvt_ref[...], hkv).astype(_BF16)

    hq = _rms(x1, gb_ref[...]).astype(_BF16)
    q = _head_rms(_dot(hq, wq_ref[...]), ones_ref, gq_ref[...])
    q_ref[...] = (q * (HEAD_DIM ** -0.5 * _LOG2_E)).astype(_BF16)


def _mlp_qkv(x, seq_len, g_mlp, w_up, w_down, g_kv, w_k, w_vt, g_k, g_b, w_q,
             g_q, ones):
    m, d = x.shape
    d_ff = w_up.shape[1]
    nq = w_q.shape[1]
    nk = w_k.shape[1]
    nv = w_vt.shape[0]
    tm = ROW_TILE
    assert seq_len % tm == 0 and m % seq_len == 0 and d_ff % FF_CHUNK == 0
    tiles_per_seq = seq_len // tm

    def rows(n):
        return pl.BlockSpec((tm, n), lambda i: (i, 0))

    return pl.pallas_call(
        _mlp_qkv_kernel,
        grid=(m // tm,),
        in_specs=[rows(d), _const_spec((1, d)), _const_spec((d, d_ff)),
                  _const_spec((d_ff, d)), _const_spec((1, d)),
                  _const_spec((d, nk)), _const_spec((nv, d)),
                  _const_spec((1, nk)),
                  _const_spec((1, d)), _const_spec((d, nq)),
                  _const_spec((1, nq)),
                  _const_spec((V7X_MXU_DIM, V7X_MXU_DIM))],
        out_specs=[rows(d), rows(nq), rows(nk),
                   pl.BlockSpec((1, nv, tm), lambda i: (i // tiles_per_seq, 0,
                                                        i % tiles_per_seq))],
        out_shape=[jax.ShapeDtypeStruct((m, d), _F32),
                   jax.ShapeDtypeStruct((m, nq), _BF16),
                   jax.ShapeDtypeStruct((m, nk), _BF16),
                   jax.ShapeDtypeStruct((m // seq_len, nv, seq_len), _BF16)],
        compiler_params=pltpu.CompilerParams(
            dimension_semantics=("arbitrary",),
            vmem_limit_bytes=V7X_VMEM_LIMIT_BYTES),
        name="mlp_qkv",
    )(x, g_mlp, w_up, w_down, g_kv, w_k, w_vt, g_k, g_b, w_q, g_q, ones)


def _attn_kernel(q_ref, k_ref, vt_ref, lq1_ref, lk1_ref, lq2_ref, lk2_ref,
                 sub_ref, o_ref, s_scr, p_scr, vx_scr, q2_scr, *, lam_init):
    s_len = q_ref.shape[1]
    tq, tk = Q_TILE, K_TILE
    lam = (jnp.exp(jnp.sum(lq1_ref[...] * lk1_ref[...], keepdims=True))
           - jnp.exp(jnp.sum(lq2_ref[...] * lk2_ref[...], keepdims=True))
           + lam_init)
    out_gain = sub_ref[...] * (1.0 - lam_init)

    vx_scr[0:V_DIM, :] = vt_ref[0]
    vx_scr[V_DIM:V_DIM + ONES_ROWS, :] = jnp.ones((ONES_ROWS, s_len), _BF16)

    lane = lax.broadcasted_iota(jnp.int32, (tq, V_DIM), 1)
    key_id = lax.broadcasted_iota(jnp.int32, (tk, 2 * tq), 0)
    qry_id = lax.broadcasted_iota(jnp.int32, (tk, 2 * tq), 1) & (tq - 1)

    tiles = list(reversed(range(s_len // tq)))
    col_max = {}

    def key_tiles(qi):
        return range(0, qi * tq + tq, tk)

    for qi in tiles:
        q = q_ref[0, qi * tq:qi * tq + tq, :]
        zero = jnp.zeros_like(q)
        q2_scr[qi, 0:tq, :] = jnp.where(lane < HEAD_DIM, q, zero)
        q2_scr[qi, tq:2 * tq, :] = jnp.where(lane >= HEAD_DIM, q, zero)

    def scores(pos):
        qi, slot, q_lo = tiles[pos], pos % SCORE_SLOTS, tiles[pos] * tq
        m = jnp.full((1, 2 * tq), NEG_INF, _F32)
        for k_lo in key_tiles(qi):
            s = _dot_nt(k_ref[0, k_lo:k_lo + tk, :], q2_scr[qi])
            if k_lo + tk > q_lo:
                visible = (k_lo + key_id) <= (q_lo + qry_id)
                s = jnp.where(visible, s, NEG_INF)
            s_scr[slot, k_lo:k_lo + tk, :] = s
            m = jnp.maximum(m, jnp.max(s, axis=0, keepdims=True))
            col_max[pos] = m
            yield

    def probs(pos):
        qi, slot = tiles[pos], pos % SCORE_SLOTS
        for k_lo in key_tiles(qi):
            p = jnp.exp2(s_scr[slot, k_lo:k_lo + tk, :] - col_max[pos])
            p_scr[slot, k_lo:k_lo + tk, :] = p.astype(_BF16)
            yield

    def values(pos):
        qi, slot, q_lo = tiles[pos], pos % SCORE_SLOTS, tiles[pos] * tq
        pv = jnp.zeros((V_DIM + ONES_ROWS, 2 * tq), _F32)
        for k_lo in key_tiles(qi):
            pv = pv + _dot(vx_scr[:, k_lo:k_lo + tk], p_scr[slot, k_lo:k_lo + tk, :])
            yield
        heads = pv[0:V_DIM] / pv[V_DIM:V_DIM + 1]
        o_t = heads[:, 0:tq] - lam * heads[:, tq:2 * tq]
        ms = jnp.mean(o_t * o_t, axis=0, keepdims=True)
        o = (o_t * lax.rsqrt(ms + EPS)).T * out_gain
        o_ref[0, q_lo:q_lo + tq, :] = o.astype(_BF16)

    for step in range(len(tiles) + 2):
        active = [stage(step - lag)
                  for lag, stage in enumerate((scores, probs, values))
                  if 0 <= step - lag < len(tiles)]
        while active:
            for gen in list(active):
                if next(gen, StopIteration) is StopIteration:
                    active.remove(gen)


def _attention(q, k, vt, lq1, lk1, lq2, lk2, sub, lam_init):
    b, s, n = q.shape
    n_heads = n // V_DIM
    assert s % Q_TILE == 0 and s % K_TILE == 0
    assert Q_TILE & (Q_TILE - 1) == 0, "query tile must be a power of two"
    head = pl.BlockSpec((1, s, V_DIM), lambda i, h: (i, 0, h))
    head_t = pl.BlockSpec((1, V_DIM, s), lambda i, h: (i, h, 0))
    vec = _const_spec((1, HEAD_DIM))
    return pl.pallas_call(
        functools.partial(_attn_kernel, lam_init=lam_init),
        grid=(b, n_heads),
        in_specs=[head, head, head_t, vec, vec, vec, vec,
                  _const_spec((1, V_DIM))],
        out_specs=head,
        out_shape=jax.ShapeDtypeStruct((b, s, n), _BF16),
        scratch_shapes=[pltpu.VMEM((SCORE_SLOTS, s, 2 * Q_TILE),_F32),
                        pltpu.VMEM((SCORE_SLOTS, s, 2 * Q_TILE),_BF16),
                        pltpu.VMEM((V_DIM + ONES_ROWS, s), _BF16),
                        pltpu.VMEM((s // Q_TILE, 2 * Q_TILE, V_DIM), _BF16)],
        compiler_params=pltpu.CompilerParams(
            dimension_semantics=("arbitrary", "arbitrary"),
            vmem_limit_bytes=V7X_VMEM_LIMIT_BYTES),
        name="attention",
    )(q, k, vt, lq1, lk1, lq2, lk2, sub)


def _out_mlp_kernel(x_ref, o_ref, wo_ref, gm_ref, wup_ref, wdn_ref, y_ref):
    x2 = x_ref[...] + _dot(o_ref[...], wo_ref[...])
    y_ref[...] = _mlp(x2, gm_ref[...], wup_ref, wdn_ref)


def _out_mlp(x, o, w_o, g_mlp, w_up, w_down):
    m, d = x.shape
    n = o.shape[1]
    d_ff = w_up.shape[1]
    tm = ROW_TILE
    assert m % tm == 0 and d_ff % FF_CHUNK == 0
    return pl.pallas_call(
        _out_mlp_kernel,
        grid=(m // tm,),
        in_specs=[pl.BlockSpec((tm, d), lambda i: (i, 0)),
                  pl.BlockSpec((tm, n), lambda i: (i, 0)),
                  _const_spec((n, d)), _const_spec((1, d)),
                  _const_spec((d, d_ff)), _const_spec((d_ff, d))],
        out_specs=pl.BlockSpec((tm, d), lambda i: (i, 0)),
        out_shape=jax.ShapeDtypeStruct((m, d), _F32),
        compiler_params=pltpu.CompilerParams(
            dimension_semantics=("arbitrary",),
            vmem_limit_bytes=V7X_VMEM_LIMIT_BYTES),
        name="out_mlp",
    )(x, o, w_o, g_mlp, w_up, w_down)


def _lambda_init(layer_idx):
    return 0.8 - 0.6 * math.exp(-0.3 * layer_idx)


def kernel(x, a_norm, a_w_in, a_conv, a_w_out, kv_norm, w_kv, k_norm, b_norm,
           w_q, q_norm, lam_q1, lam_k1, lam_q2, lam_k2, sub_norm, w_o,
           mlp_norm, w_up, w_down):
    b, s, d = x.shape
    n_a, n_b = a_norm.shape[0], b_norm.shape[0]
    assert n_a == 1 and n_b == 1, "one mixer layer followed by one attention layer"
    n_qk = w_q.shape[2]
    assert w_kv.shape[1] == 2 * n_qk and n_qk % V7X_MXU_DIM == 0

    bf = lambda w: w.astype(_BF16)
    row = lambda g: g.reshape(1, -1)
    groups = n_qk // HEAD_DIM
    gid = jnp.arange(V7X_MXU_DIM) // HEAD_DIM
    ones = (gid[:, None] == gid[None, :]).astype(_BF16)

    x = _mixer(x, row(a_norm[0]), bf(a_w_in[0]), a_conv[0], bf(a_w_out[0]))
    x1, q, k, vt = _mlp_qkv(
        x.reshape(b * s, d), s, row(mlp_norm[0]), bf(w_up[0]), bf(w_down[0]),
        row(kv_norm), bf(w_kv[:, :n_qk]), bf(w_kv[:, n_qk:].T),
        row(jnp.tile(k_norm, groups)),
        row(b_norm[0]), bf(w_q[0]), row(jnp.tile(q_norm[0], groups)), ones)
    o = _attention(q.reshape(b, s, n_qk), k.reshape(b, s, n_qk),
                   vt, row(lam_q1[0]), row(lam_k1[0]),
                   row(lam_q2[0]), row(lam_k2[0]), row(sub_norm[0]),
                   _lambda_init(n_a))
    y = _out_mlp(x1, o.reshape(b * s, -1), bf(w_o[0]), row(mlp_norm[1]),
                 bf(w_up[1]), bf(w_down[1]))
    return y.reshape(b, s, d)
```

```python
import functools
import math

import jax
import jax.numpy as jnp
from jax import lax
from jax.experimental import pallas as pl
from jax.experimental.pallas import tpu as pltpu

EPS = 1e-6
NEG_INF = -1e30
HEAD_DIM = 64
V_DIM = 2 * HEAD_DIM
CONV_WIDTH = 3

V7X_LANES = 128
V7X_SUBLANES = 8
V7X_MXU_DIM = 256
V7X_BF16_ROWS_PER_VREG = 16
V7X_VMEM_LIMIT_BYTES = 60 * 1024 * 1024

ROW_TILE = 512
FF_CHUNK = 1024
Q_TILE = 256
K_TILE = 256
CAST_ROWS = 128

_LOG2_E = math.log2(math.e)
_BF16 = jnp.bfloat16
_F32 = jnp.float32


def _dot(a, b):
    return jnp.dot(a, b, preferred_element_type=_F32)


def _dot_nt(a, b):
    return lax.dot_general(a, b, (((1,), (1,)), ((), ())),
                           preferred_element_type=_F32)


def _rms(x, g):
    ms = jnp.mean(x * x, axis=-1, keepdims=True)
    return x * lax.rsqrt(ms + EPS) * g


def _mlp(x, g, wup_ref, wdn_ref):
    h = _rms(x, g).astype(_BF16)
    acc = x
    d_ff = wup_ref.shape[1]
    for c in range(d_ff // FF_CHUNK):
        sl = slice(c * FF_CHUNK, (c + 1) * FF_CHUNK)
        a = jnp.maximum(_dot(h, wup_ref[:, sl]), 0.0)
        acc = acc + _dot((a * a).astype(_BF16), wdn_ref[sl, :])
    return acc


def _head_rms(y, ones_ref, gain):
    y2 = (y * y).astype(_BF16)
    n = y.shape[1]
    parts = [_dot(y2[:, t:t + V7X_MXU_DIM], ones_ref[...])
             for t in range(0, n, V7X_MXU_DIM)]
    ss = jnp.concatenate(parts, axis=1)
    return y * lax.rsqrt(ss * (1.0 / HEAD_DIM) + EPS) * gain


def _const_spec(shape):
    return pl.BlockSpec(shape, lambda *_: (0,) * len(shape),
                        pipeline_mode=pl.Buffered(1))


def _layer_spec(w, layer):
    _, r, c = w.shape
    return pl.BlockSpec((None, r, c), lambda *_: (layer, 0, 0),
                        pipeline_mode=pl.Buffered(1))


def _cast_jobs(weights, n_steps, step_of):
    in_specs, out_specs, shapes = [], [], []
    for w, layer in weights:
        _, r, c = w.shape
        assert r % (n_steps * V7X_BF16_ROWS_PER_VREG) == 0
        block = (None, r // n_steps, c)
        in_specs.append(pl.BlockSpec(
            block, lambda *ids, layer=layer: (layer, step_of(*ids), 0)))
        out_specs.append(pl.BlockSpec(
            block, lambda *ids: (0, step_of(*ids), 0)))
        shapes.append(jax.ShapeDtypeStruct((1, r, c), _BF16))
    return in_specs, out_specs, shapes


def _run_casts(src_refs, dst_refs):
    for src, dst in zip(src_refs, dst_refs):
        dst[...] = src[...].astype(_BF16)


def _mixer_kernel(*refs, n_cast):
    x_ref, g_ref, win_ref, conv_ref, wout_ref = refs[:5]
    cast_src = refs[5:5 + n_cast]
    o_ref = refs[5 + n_cast]
    cast_dst = refs[6 + n_cast:6 + 2 * n_cast]
    carry_ref, win_bf, wout_bf = refs[6 + 2 * n_cast:]
    d = x_ref.shape[2]
    ts = x_ref.shape[1]

    @pl.when((pl.program_id(0) == 0) & (pl.program_id(1) == 0))
    def _():
        def cast_rows(i, _):
            r = pl.ds(pl.multiple_of(i * CAST_ROWS, CAST_ROWS), CAST_ROWS)
            win_bf[r, :] = win_ref[r, :].astype(_BF16)
            wout_bf[r, :] = wout_ref[r, :].astype(_BF16)
            return 0
        lax.fori_loop(0, d // CAST_ROWS, cast_rows, 0)

    @pl.when(pl.program_id(1) == 0)
    def _():
        carry_ref[...] = jnp.zeros_like(carry_ref)

    _run_casts(cast_src, cast_dst)

    x = x_ref[0]
    h = _rms(x, g_ref[...]).astype(_BF16)
    b_gate = _dot(h, win_bf[:, 0:d])
    c_gate = _dot(h, win_bf[:, d:2 * d])
    v = _dot(h, win_bf[:, 2 * d:3 * d])
    u = c_gate * v

    rows = lax.broadcasted_iota(jnp.int32, u.shape, 0)
    tail = carry_ref[...]
    prev1 = tail[V7X_SUBLANES - 1:V7X_SUBLANES]
    prev2 = tail[V7X_SUBLANES - 2:V7X_SUBLANES - 1]
    u1 = jnp.where(rows == 0, prev1, pltpu.roll(u, 1, 0))
    u2 = jnp.where(rows == 0, prev2,
                   jnp.where(rows == 1, prev1, pltpu.roll(u, 2, 0)))
    w = conv_ref[...]
    y = w[0:1] * u2 + w[1:2] * u1 + w[2:3] * u
    carry_ref[...] = u[ts - V7X_SUBLANES:ts]

    o_ref[0] = x + _dot((b_gate * y).astype(_BF16), wout_bf[...])


def _mixer(x, g, w_in, conv_w, w_out, layer, next_weights):
    b, s, d = x.shape
    ts = ROW_TILE
    assert s % ts == 0 and d % CAST_ROWS == 0
    n_seq = s // ts
    tok = pl.BlockSpec((1, ts, d), lambda i, j: (i, j, 0))
    cast_in, cast_out, cast_shapes = _cast_jobs(next_weights, b * n_seq,
                                                lambda i, j: i * n_seq + j)
    outs = pl.pallas_call(
        functools.partial(_mixer_kernel, n_cast=len(cast_in)),
        grid=(b, n_seq),
        in_specs=[tok, _const_spec((1, d)), _layer_spec(w_in, layer),
                  _const_spec((CONV_WIDTH, d)), _layer_spec(w_out, layer),
                  *cast_in],
        out_specs=[tok, *cast_out],
        out_shape=[jax.ShapeDtypeStruct(x.shape, _F32), *cast_shapes],
        scratch_shapes=[pltpu.VMEM((V7X_SUBLANES, d), _F32),
                        pltpu.VMEM(w_in.shape[1:], _BF16),
                        pltpu.VMEM(w_out.shape[1:], _BF16)],
        compiler_params=pltpu.CompilerParams(
            dimension_semantics=("arbitrary", "arbitrary"),
            vmem_limit_bytes=V7X_VMEM_LIMIT_BYTES),
        name="mixer",
    )(x, g, w_in, conv_w, w_out, *[w for w, _ in next_weights])
    return outs[0], outs[1:]


def _mlp_qkv_kernel(*refs, n_cast):
    (x_ref, gm_ref, wup_ref, wdn_ref, gkv_ref, wkv_ref, gk_ref, gb_ref, wq_ref,
     gq_ref, ones_ref) = refs[:11]
    cast_src = refs[11:11 + n_cast]
    x1_ref, q_ref, k_ref, v_ref = refs[11 + n_cast:15 + n_cast]
    cast_dst = refs[15 + n_cast:]

    _run_casts(cast_src, cast_dst)

    x1 = _mlp(x_ref[...], gm_ref[...], wup_ref, wdn_ref)
    x1_ref[...] = x1

    nk = k_ref.shape[1]
    hkv = _rms(x1, gkv_ref[...]).astype(_BF16)
    k = _dot(hkv, wkv_ref[:, 0:nk])
    k_ref[...] = _head_rms(k, ones_ref, gk_ref[...]).astype(_BF16)
    v_ref[...] = _dot(hkv, wkv_ref[:, nk:]).astype(_BF16)

    hq = _rms(x1, gb_ref[...]).astype(_BF16)
    q = _head_rms(_dot(hq, wq_ref[...]), ones_ref, gq_ref[...])
    q_ref[...] = (q * (HEAD_DIM ** -0.5 * _LOG2_E)).astype(_BF16)


def _mlp_qkv(x, g_mlp, w_up, w_down, g_kv, w_kv, g_k, g_b, w_q, g_q, ones,
             next_weights):
    m, d = x.shape
    d_ff = w_up.shape[2]
    nq = w_q.shape[2]
    nkv = w_kv.shape[2]
    nk = nq
    tm = ROW_TILE
    assert m % tm == 0 and d_ff % FF_CHUNK == 0

    def rows(n):
        return pl.BlockSpec((tm, n), lambda i: (i, 0))

    cast_in, cast_out, cast_shapes = _cast_jobs(next_weights, m // tm,
                                                lambda i: i)
    outs = pl.pallas_call(
        functools.partial(_mlp_qkv_kernel, n_cast=len(cast_in)),
        grid=(m // tm,),
        in_specs=[rows(d), _const_spec((1, d)), _layer_spec(w_up, 0),
                  _layer_spec(w_down, 0), _const_spec((1, d)),
                  _layer_spec(w_kv, 0), _const_spec((1, nk)),
                  _const_spec((1, d)), _layer_spec(w_q, 0),
                  _const_spec((1, nq)),
                  _const_spec((V7X_MXU_DIM, V7X_MXU_DIM)), *cast_in],
        out_specs=[rows(d), rows(nq), rows(nk), rows(nkv - nk), *cast_out],
        out_shape=[jax.ShapeDtypeStruct((m, d), _F32),
                   jax.ShapeDtypeStruct((m, nq), _BF16),
                   jax.ShapeDtypeStruct((m, nk), _BF16),
                   jax.ShapeDtypeStruct((m, nkv - nk), _BF16), *cast_shapes],
        compiler_params=pltpu.CompilerParams(
            dimension_semantics=("arbitrary",),
            vmem_limit_bytes=V7X_VMEM_LIMIT_BYTES),
        name="mlp_qkv",
    )(x, g_mlp, w_up, w_down, g_kv, w_kv, g_k, g_b, w_q, g_q, ones,
      *[w for w, _ in next_weights])
    return outs[:4], outs[4:]


def _attn_kernel(q_ref, k_ref, v_ref, lq1_ref, lk1_ref, lq2_ref, lk2_ref,
                 sub_ref, o_ref, s_scr, p_scr, vx_scr, *, lam_init):
    s_len = q_ref.shape[1]
    tq, tk = Q_TILE, K_TILE
    lam = (jnp.exp(jnp.sum(lq1_ref[...] * lk1_ref[...], keepdims=True))
           - jnp.exp(jnp.sum(lq2_ref[...] * lk2_ref[...], keepdims=True))
           + lam_init)
    out_gain = sub_ref[...] * (1.0 - lam_init)

    vx_scr[:, 0:V_DIM] = v_ref[0]
    vx_scr[:, V_DIM:2 * V_DIM] = jnp.ones((s_len, V_DIM), _BF16)

    lane = lax.broadcasted_iota(jnp.int32, (tq, V_DIM), 1)
    row_id = lax.broadcasted_iota(jnp.int32, (tq, tk), 0)
    row_id = jnp.concatenate([row_id, row_id], axis=0)
    col_id = lax.broadcasted_iota(jnp.int32, (2 * tq, tk), 1)

    def lane_tiles(a):
        return [a[:, t:t + V7X_LANES] for t in range(0, a.shape[1], V7X_LANES)]

    for qi in reversed(range(s_len // tq)):
        slot = qi % 2
        q_lo = qi * tq
        n_keys = q_lo + tq
        q = q_ref[0, q_lo:q_lo + tq, :]
        zero = jnp.zeros_like(q)
        q2 = jnp.concatenate([jnp.where(lane < HEAD_DIM, q, zero),
                              jnp.where(lane >= HEAD_DIM, q, zero)], axis=0)
        m_run = jnp.full((2 * tq, V7X_LANES), NEG_INF, _F32)
        for k_lo in range(0, n_keys, tk):
            s = _dot_nt(q2, k_ref[0, k_lo:k_lo + tk, :])
            if k_lo + tk > q_lo:
                visible = (k_lo + col_id) <= (q_lo + row_id)
                s = jnp.where(visible, s, NEG_INF)
            s_scr[slot, :, k_lo:k_lo + tk] = s
            for part in lane_tiles(s):
                m_run = jnp.maximum(m_run, part)
        m = jnp.max(m_run, axis=-1, keepdims=True)
        for k_lo in range(0, n_keys, tk):
            p = jnp.exp2(s_scr[slot, :, k_lo:k_lo + tk] - m)
            p_scr[slot, :, k_lo:k_lo + tk] = p.astype(_BF16)
        pv = _dot(p_scr[slot, :, 0:n_keys], vx_scr[0:n_keys, :])
        heads = pv[:, 0:V_DIM] / pv[:, V_DIM:2 * V_DIM]
        o = heads[0:tq] - lam * heads[tq:2 * tq]
        o_ref[0, q_lo:q_lo + tq, :] = (_rms(o, out_gain)).astype(_BF16)


def _attention(q, k, v, lq1, lk1, lq2, lk2, sub, lam_init):
    b, s, n = q.shape
    n_heads = n // V_DIM
    assert s % Q_TILE == 0 and s % K_TILE == 0
    head = pl.BlockSpec((1, s, V_DIM), lambda i, h: (i, 0, h))
    vec = _const_spec((1, HEAD_DIM))
    return pl.pallas_call(
        functools.partial(_attn_kernel, lam_init=lam_init),
        grid=(b, n_heads),
        in_specs=[head, head, head, vec, vec, vec, vec, _const_spec((1, V_DIM))],
        out_specs=head,
        out_shape=jax.ShapeDtypeStruct((b, s, n), _BF16),
        scratch_shapes=[pltpu.VMEM((2, 2 * Q_TILE, s), _F32),
                        pltpu.VMEM((2, 2 * Q_TILE, s), _BF16),
                        pltpu.VMEM((s, 2 * V_DIM), _BF16)],
        compiler_params=pltpu.CompilerParams(
            dimension_semantics=("arbitrary", "arbitrary"),
            vmem_limit_bytes=V7X_VMEM_LIMIT_BYTES),
        name="attention",
    )(q, k, v, lq1, lk1, lq2, lk2, sub)


def _out_mlp_kernel(x_ref, o_ref, wo_ref, gm_ref, wup_ref, wdn_ref, y_ref):
    x2 = x_ref[...] + _dot(o_ref[...], wo_ref[...])
    y_ref[...] = _mlp(x2, gm_ref[...], wup_ref, wdn_ref)


def _out_mlp(x, o, w_o, g_mlp, w_up, w_down):
    m, d = x.shape
    n = o.shape[1]
    d_ff = w_up.shape[2]
    tm = ROW_TILE
    assert m % tm == 0 and d_ff % FF_CHUNK == 0
    return pl.pallas_call(
        _out_mlp_kernel,
        grid=(m // tm,),
        in_specs=[pl.BlockSpec((tm, d), lambda i: (i, 0)),
                  pl.BlockSpec((tm, n), lambda i: (i, 0)),
                  _layer_spec(w_o, 0), _const_spec((1, d)),
                  _layer_spec(w_up, 0), _layer_spec(w_down, 0)],
        out_specs=pl.BlockSpec((tm, d), lambda i: (i, 0)),
        out_shape=jax.ShapeDtypeStruct((m, d), _F32),
        compiler_params=pltpu.CompilerParams(
            dimension_semantics=("arbitrary",),
            vmem_limit_bytes=V7X_VMEM_LIMIT_BYTES),
        name="out_mlp",
    )(x, o, w_o, g_mlp, w_up, w_down)


def _lambda_init(layer_idx):
    return 0.8 - 0.6 * math.exp(-0.3 * layer_idx)


def kernel(x, a_norm, a_w_in, a_conv, a_w_out, kv_norm, w_kv, k_norm, b_norm,
           w_q, q_norm, lam_q1, lam_k1, lam_q2, lam_k2, sub_norm, w_o,
           mlp_norm, w_up, w_down):
    b, s, d = x.shape
    n_a, n_b = a_norm.shape[0], b_norm.shape[0]
    assert n_a == 1 and n_b == 1, "one mixer layer followed by one attention layer"
    n_qk = w_q.shape[2]
    assert w_kv.shape[1] == 2 * n_qk and n_qk % V7X_MXU_DIM == 0

    row = lambda g: g.reshape(1, -1)
    groups = n_qk // HEAD_DIM
    gid = jnp.arange(V7X_MXU_DIM) // HEAD_DIM
    ones = (gid[:, None] == gid[None, :]).astype(_BF16)

    x, (w_up_bf0, w_down_bf0, w_kv_bf, w_q_bf) = _mixer(
        x, row(a_norm[0]), a_w_in, a_conv[0], a_w_out, 0,
        [(w_up, 0), (w_down, 0), (w_kv[None], 0), (w_q, 0)])
    (x1, q, k, v), (w_o_bf, w_up_bf1, w_down_bf1) = _mlp_qkv(
        x.reshape(b * s, d), row(mlp_norm[0]), w_up_bf0, w_down_bf0,
        row(kv_norm), w_kv_bf, row(jnp.tile(k_norm, groups)),
        row(b_norm[0]), w_q_bf, row(jnp.tile(q_norm[0], groups)), ones,
        [(w_o, 0), (w_up, 1), (w_down, 1)])
    o = _attention(q.reshape(b, s, n_qk), k.reshape(b, s, n_qk),
                   v.reshape(b, s, -1), row(lam_q1[0]), row(lam_k1[0]),
                   row(lam_q2[0]), row(lam_k2[0]), row(sub_norm[0]),
                   _lambda_init(n_a))
    y = _out_mlp(x1, o.reshape(b * s, -1), w_o_bf, row(mlp_norm[1]),
                 w_up_bf1, w_down_bf1)
    return y.reshape(b, s, d)
```

```python
import functools
import math

import jax
import jax.numpy as jnp
from jax import lax
from jax.experimental import pallas as pl
from jax.experimental.pallas import tpu as pltpu

EPS = 1e-6
NEG_INF = -1e30
HEAD_DIM = 64
V_DIM = 2 * HEAD_DIM
CONV_WIDTH = 3

V7X_LANES = 128
V7X_SUBLANES = 8
V7X_MXU_DIM = 256
V7X_BF16_ROWS_PER_VREG = 16
V7X_VMEM_LIMIT_BYTES = 60 * 1024 * 1024

ROW_TILE = 512
FF_CHUNK = 1024
Q_TILE = 256
K_TILE = 256
CAST_ROWS = 128

_LOG2_E = math.log2(math.e)
_BF16 = jnp.bfloat16
_F32 = jnp.float32


def _dot(a, b):
    return jnp.dot(a, b, preferred_element_type=_F32)


def _dot_nt(a, b):
    return lax.dot_general(a, b, (((1,), (1,)), ((), ())),
                           preferred_element_type=_F32)


def _rms(x, g):
    ms = jnp.mean(x * x, axis=-1, keepdims=True)
    return x * lax.rsqrt(ms + EPS) * g


def _mlp(x, g, wup_ref, wdn_ref):
    h = _rms(x, g).astype(_BF16)
    acc = x
    d_ff = wup_ref.shape[1]
    for c in range(d_ff // FF_CHUNK):
        sl = slice(c * FF_CHUNK, (c + 1) * FF_CHUNK)
        a = jnp.maximum(_dot(h, wup_ref[:, sl]), 0.0)
        acc = acc + _dot((a * a).astype(_BF16), wdn_ref[sl, :])
    return acc


def _head_rms(y, ones_ref, gain):
    y2 = (y * y).astype(_BF16)
    n = y.shape[1]
    parts = [_dot(y2[:, t:t + V7X_MXU_DIM], ones_ref[...])
             for t in range(0, n, V7X_MXU_DIM)]
    ss = jnp.concatenate(parts, axis=1)
    return y * lax.rsqrt(ss * (1.0 / HEAD_DIM) + EPS) * gain


def _const_spec(shape):
    return pl.BlockSpec(shape, lambda *_: (0,) * len(shape),
                        pipeline_mode=pl.Buffered(1))


def _layer_spec(w, layer):
    _, r, c = w.shape
    return pl.BlockSpec((None, r, c), lambda *_: (layer, 0, 0),
                        pipeline_mode=pl.Buffered(1))


def _cast_jobs(weights, n_steps, step_of):
    in_specs, out_specs, shapes = [], [], []
    for w, layer in weights:
        _, r, c = w.shape
        assert r % (n_steps * V7X_BF16_ROWS_PER_VREG) == 0
        block = (None, r // n_steps, c)
        in_specs.append(pl.BlockSpec(
            block, lambda *ids, layer=layer: (layer, step_of(*ids), 0)))
        out_specs.append(pl.BlockSpec(
            block, lambda *ids: (0, step_of(*ids), 0)))
        shapes.append(jax.ShapeDtypeStruct((1, r, c), _BF16))
    return in_specs, out_specs, shapes


def _run_casts(src_refs, dst_refs):
    for src, dst in zip(src_refs, dst_refs):
        dst[...] = src[...].astype(_BF16)


def _mixer_kernel(*refs, n_cast):
    x_ref, g_ref, win_ref, conv_ref, wout_ref = refs[:5]
    cast_src = refs[5:5 + n_cast]
    o_ref = refs[5 + n_cast]
    cast_dst = refs[6 + n_cast:6 + 2 * n_cast]
    carry_ref, win_bf, wout_bf = refs[6 + 2 * n_cast:]
    d = x_ref.shape[2]
    ts = x_ref.shape[1]

    @pl.when((pl.program_id(0) == 0) & (pl.program_id(1) == 0))
    def _():
        def cast_rows(i, _):
            r = pl.ds(pl.multiple_of(i * CAST_ROWS, CAST_ROWS), CAST_ROWS)
            win_bf[r, :] = win_ref[r, :].astype(_BF16)
            wout_bf[r, :] = wout_ref[r, :].astype(_BF16)
            return 0
        lax.fori_loop(0, d // CAST_ROWS, cast_rows, 0)

    @pl.when(pl.program_id(1) == 0)
    def _():
        carry_ref[...] = jnp.zeros_like(carry_ref)

    _run_casts(cast_src, cast_dst)

    x = x_ref[0]
    h = _rms(x, g_ref[...]).astype(_BF16)
    b_gate = _dot(h, win_bf[:, 0:d])
    c_gate = _dot(h, win_bf[:, d:2 * d])
    v = _dot(h, win_bf[:, 2 * d:3 * d])
    u = c_gate * v

    rows = lax.broadcasted_iota(jnp.int32, u.shape, 0)
    tail = carry_ref[...]
    prev1 = tail[V7X_SUBLANES - 1:V7X_SUBLANES]
    prev2 = tail[V7X_SUBLANES - 2:V7X_SUBLANES - 1]
    u1 = jnp.where(rows == 0, prev1, pltpu.roll(u, 1, 0))
    u2 = jnp.where(rows == 0, prev2,
                   jnp.where(rows == 1, prev1, pltpu.roll(u, 2, 0)))
    w = conv_ref[...]
    y = w[0:1] * u2 + w[1:2] * u1 + w[2:3] * u
    carry_ref[...] = u[ts - V7X_SUBLANES:ts]

    o_ref[0] = x + _dot((b_gate * y).astype(_BF16), wout_bf[...])


def _mixer(x, g, w_in, conv_w, w_out, layer, next_weights):
    b, s, d = x.shape
    ts = ROW_TILE
    assert s % ts == 0 and d % CAST_ROWS == 0
    n_seq = s // ts
    tok = pl.BlockSpec((1, ts, d), lambda i, j: (i, j, 0))
    cast_in, cast_out, cast_shapes = _cast_jobs(next_weights, b * n_seq,
                                                lambda i, j: i * n_seq + j)
    outs = pl.pallas_call(
        functools.partial(_mixer_kernel, n_cast=len(cast_in)),
        grid=(b, n_seq),
        in_specs=[tok, _const_spec((1, d)), _layer_spec(w_in, layer),
                  _const_spec((CONV_WIDTH, d)), _layer_spec(w_out, layer),
                  *cast_in],
        out_specs=[tok, *cast_out],
        out_shape=[jax.ShapeDtypeStruct(x.shape, _F32), *cast_shapes],
        scratch_shapes=[pltpu.VMEM((V7X_SUBLANES, d), _F32),
                        pltpu.VMEM(w_in.shape[1:], _BF16),
                        pltpu.VMEM(w_out.shape[1:], _BF16)],
        compiler_params=pltpu.CompilerParams(
            dimension_semantics=("arbitrary", "arbitrary"),
            vmem_limit_bytes=V7X_VMEM_LIMIT_BYTES),
        name="mixer",
    )(x, g, w_in, conv_w, w_out, *[w for w, _ in next_weights])
    return outs[0], outs[1:]


def _mlp_qkv_kernel(*refs, n_cast):
    (x_ref, gm_ref, wup_ref, wdn_ref, gkv_ref, wkv_ref, gk_ref, gb_ref, wq_ref,
     gq_ref, ones_ref) = refs[:11]
    cast_src = refs[11:11 + n_cast]
    x1_ref, q_ref, k_ref, v_ref = refs[11 + n_cast:15 + n_cast]
    cast_dst = refs[15 + n_cast:]

    _run_casts(cast_src, cast_dst)

    x1 = _mlp(x_ref[...], gm_ref[...], wup_ref, wdn_ref)
    x1_ref[...] = x1

    nk = k_ref.shape[1]
    hkv = _rms(x1, gkv_ref[...]).astype(_BF16)
    k = _dot(hkv, wkv_ref[:, 0:nk])
    k_ref[...] = _head_rms(k, ones_ref, gk_ref[...]).astype(_BF16)
    v_ref[...] = _dot(hkv, wkv_ref[:, nk:]).astype(_BF16)

    hq = _rms(x1, gb_ref[...]).astype(_BF16)
    q = _head_rms(_dot(hq, wq_ref[...]), ones_ref, gq_ref[...])
    q_ref[...] = (q * (HEAD_DIM ** -0.5 * _LOG2_E)).astype(_BF16)


def _mlp_qkv(x, g_mlp, w_up, w_down, g_kv, w_kv, g_k, g_b, w_q, g_q, ones,
             next_weights):
    m, d = x.shape
    d_ff = w_up.shape[2]
    nq = w_q.shape[2]
    nkv = w_kv.shape[2]
    nk = nq
    tm = ROW_TILE
    assert m % tm == 0 and d_ff % FF_CHUNK == 0

    def rows(n):
        return pl.BlockSpec((tm, n), lambda i: (i, 0))

    cast_in, cast_out, cast_shapes = _cast_jobs(next_weights, m // tm,
                                                lambda i: i)
    outs = pl.pallas_call(
        functools.partial(_mlp_qkv_kernel, n_cast=len(cast_in)),
        grid=(m // tm,),
        in_specs=[rows(d), _const_spec((1, d)), _layer_spec(w_up, 0),
                  _layer_spec(w_down, 0), _const_spec((1, d)),
                  _layer_spec(w_kv, 0), _const_spec((1, nk)),
                  _const_spec((1, d)), _layer_spec(w_q, 0),
                  _const_spec((1, nq)),
                  _const_spec((V7X_MXU_DIM, V7X_MXU_DIM)), *cast_in],
        out_specs=[rows(d), rows(nq), rows(nk), rows(nkv - nk), *cast_out],
        out_shape=[jax.ShapeDtypeStruct((m, d), _F32),
                   jax.ShapeDtypeStruct((m, nq), _BF16),
                   jax.ShapeDtypeStruct((m, nk), _BF16),
                   jax.ShapeDtypeStruct((m, nkv - nk), _BF16), *cast_shapes],
        compiler_params=pltpu.CompilerParams(
            dimension_semantics=("arbitrary",),
            vmem_limit_bytes=V7X_VMEM_LIMIT_BYTES),
        name="mlp_qkv",
    )(x, g_mlp, w_up, w_down, g_kv, w_kv, g_k, g_b, w_q, g_q, ones,
      *[w for w, _ in next_weights])
    return outs[:4], outs[4:]


def _attn_kernel(q_ref, k_ref, v_ref, lq1_ref, lk1_ref, lq2_ref, lk2_ref,
                 sub_ref, o_ref, s_scr, p_scr, vx_scr, *, lam_init):
    s_len = q_ref.shape[1]
    tq, tk = Q_TILE, K_TILE
    lam = (jnp.exp(jnp.sum(lq1_ref[...] * lk1_ref[...], keepdims=True))
           - jnp.exp(jnp.sum(lq2_ref[...] * lk2_ref[...], keepdims=True))
           + lam_init)
    out_gain = sub_ref[...] * (1.0 - lam_init)

    vx_scr[:, 0:V_DIM] = v_ref[0]
    vx_scr[:, V_DIM:2 * V_DIM] = jnp.ones((s_len, V_DIM), _BF16)

    lane = lax.broadcasted_iota(jnp.int32, (tq, V_DIM), 1)
    row_id = lax.broadcasted_iota(jnp.int32, (tq, tk), 0)
    row_id = jnp.concatenate([row_id, row_id], axis=0)
    col_id = lax.broadcasted_iota(jnp.int32, (2 * tq, tk), 1)

    def lane_tiles(a):
        return [a[:, t:t + V7X_LANES] for t in range(0, a.shape[1], V7X_LANES)]

    tiles = list(reversed(range(s_len // tq)))
    row_max = {}

    def scores(pos):
        qi, slot = tiles[pos], pos % 2
        q_lo = qi * tq
        q = q_ref[0, q_lo:q_lo + tq, :]
        zero = jnp.zeros_like(q)
        q2 = jnp.concatenate([jnp.where(lane < HEAD_DIM, q, zero),
                              jnp.where(lane >= HEAD_DIM, q, zero)], axis=0)
        m_run = jnp.full((2 * tq, V7X_LANES), NEG_INF, _F32)
        for k_lo in range(0, q_lo + tq, tk):
            s = _dot_nt(q2, k_ref[0, k_lo:k_lo + tk, :])
            if k_lo + tk > q_lo:
                visible = (k_lo + col_id) <= (q_lo + row_id)
                s = jnp.where(visible, s, NEG_INF)
            s_scr[slot, :, k_lo:k_lo + tk] = s
            for part in lane_tiles(s):
                m_run = jnp.maximum(m_run, part)
            yield
        row_max[pos] = jnp.max(m_run, axis=-1, keepdims=True)

    def probs(pos):
        qi, slot = tiles[pos], pos % 2
        for k_lo in range(0, qi * tq + tq, tk):
            p = jnp.exp2(s_scr[slot, :, k_lo:k_lo + tk] - row_max[pos])
            p_scr[slot, :, k_lo:k_lo + tk] = p.astype(_BF16)
            yield

    def values(pos):
        qi, slot = tiles[pos], pos % 2
        q_lo, n_keys = qi * tq, qi * tq + tq
        pv = _dot(p_scr[slot, :, 0:n_keys], vx_scr[0:n_keys, :])
        heads = pv[:, 0:V_DIM] / pv[:, V_DIM:2 * V_DIM]
        o = heads[0:tq] - lam * heads[tq:2 * tq]
        o_ref[0, q_lo:q_lo + tq, :] = (_rms(o, out_gain)).astype(_BF16)

    for _ in scores(0):
        pass
    for pos in range(len(tiles) + 1):
        if pos >= 1:
            values(pos - 1)
        active = [stage(p) for stage, p in ((probs, pos), (scores, pos + 1))
                  if p < len(tiles)]
        while active:
            for gen in list(active):
                if next(gen, StopIteration) is StopIteration:
                    active.remove(gen)


def _attention(q, k, v, lq1, lk1, lq2, lk2, sub, lam_init):
    b, s, n = q.shape
    n_heads = n // V_DIM
    assert s % Q_TILE == 0 and s % K_TILE == 0
    head = pl.BlockSpec((1, s, V_DIM), lambda i, h: (i, 0, h))
    vec = _const_spec((1, HEAD_DIM))
    return pl.pallas_call(
        functools.partial(_attn_kernel, lam_init=lam_init),
        grid=(b, n_heads),
        in_specs=[head, head, head, vec, vec, vec, vec, _const_spec((1, V_DIM))],
        out_specs=head,
        out_shape=jax.ShapeDtypeStruct((b, s, n), _BF16),
        scratch_shapes=[pltpu.VMEM((2, 2 * Q_TILE, s), _F32),
                        pltpu.VMEM((2, 2 * Q_TILE, s), _BF16),
                        pltpu.VMEM((s, 2 * V_DIM), _BF16)],
        compiler_params=pltpu.CompilerParams(
            dimension_semantics=("arbitrary", "arbitrary"),
            vmem_limit_bytes=V7X_VMEM_LIMIT_BYTES),
        name="attention",
    )(q, k, v, lq1, lk1, lq2, lk2, sub)


def _out_mlp_kernel(x_ref, o_ref, wo_ref, gm_ref, wup_ref, wdn_ref, y_ref):
    x2 = x_ref[...] + _dot(o_ref[...], wo_ref[...])
    y_ref[...] = _mlp(x2, gm_ref[...], wup_ref, wdn_ref)


def _out_mlp(x, o, w_o, g_mlp, w_up, w_down):
    m, d = x.shape
    n = o.shape[1]
    d_ff = w_up.shape[2]
    tm = ROW_TILE
    assert m % tm == 0 and d_ff % FF_CHUNK == 0
    return pl.pallas_call(
        _out_mlp_kernel,
        grid=(m // tm,),
        in_specs=[pl.BlockSpec((tm, d), lambda i: (i, 0)),
                  pl.BlockSpec((tm, n), lambda i: (i, 0)),
                  _layer_spec(w_o, 0), _const_spec((1, d)),
                  _layer_spec(w_up, 0), _layer_spec(w_down, 0)],
        out_specs=pl.BlockSpec((tm, d), lambda i: (i, 0)),
        out_shape=jax.ShapeDtypeStruct((m, d), _F32),
        compiler_params=pltpu.CompilerParams(
            dimension_semantics=("arbitrary",),
            vmem_limit_bytes=V7X_VMEM_LIMIT_BYTES),
        name="out_mlp",
    )(x, o, w_o, g_mlp, w_up, w_down)


def _lambda_init(layer_idx):
    return 0.8 - 0.6 * math.exp(-0.3 * layer_idx)


def kernel(x, a_norm, a_w_in, a_conv, a_w_out, kv_norm, w_kv, k_norm, b_norm,
           w_q, q_norm, lam_q1, lam_k1, lam_q2, lam_k2, sub_norm, w_o,
           mlp_norm, w_up, w_down):
    b, s, d = x.shape
    n_a, n_b = a_norm.shape[0], b_norm.shape[0]
    assert n_a == 1 and n_b == 1, "one mixer layer followed by one attention layer"
    n_qk = w_q.shape[2]
    assert w_kv.shape[1] == 2 * n_qk and n_qk % V7X_MXU_DIM == 0

    row = lambda g: g.reshape(1, -1)
    groups = n_qk // HEAD_DIM
    gid = jnp.arange(V7X_MXU_DIM) // HEAD_DIM
    ones = (gid[:, None] == gid[None, :]).astype(_BF16)

    x, (w_up_bf0, w_down_bf0, w_kv_bf, w_q_bf) = _mixer(
        x, row(a_norm[0]), a_w_in, a_conv[0], a_w_out, 0,
        [(w_up, 0), (w_down, 0), (w_kv[None], 0), (w_q, 0)])
    (x1, q, k, v), (w_o_bf, w_up_bf1, w_down_bf1) = _mlp_qkv(
        x.reshape(b * s, d), row(mlp_norm[0]), w_up_bf0, w_down_bf0,
        row(kv_norm), w_kv_bf, row(jnp.tile(k_norm, groups)),
        row(b_norm[0]), w_q_bf, row(jnp.tile(q_norm[0], groups)), ones,
        [(w_o, 0), (w_up, 1), (w_down, 1)])
    o = _attention(q.reshape(b, s, n_qk), k.reshape(b, s, n_qk),
                   v.reshape(b, s, -1), row(lam_q1[0]), row(lam_k1[0]),
                   row(lam_q2[0]), row(lam_k2[0]), row(sub_norm[0]),
                   _lambda_init(n_a))
    y = _out_mlp(x1, o.reshape(b * s, -1), w_o_bf, row(mlp_norm[1]),
                 w_up_bf1, w_down_bf1)
    return y.reshape(b, s, d)
```

```python
import functools
import math

import jax
import jax.numpy as jnp
from jax import lax
from jax.experimental import pallas as pl
from jax.experimental.pallas import tpu as pltpu

EPS = 1e-6
NEG_INF = -1e30
HEAD_DIM = 64
V_DIM = 2 * HEAD_DIM
CONV_WIDTH = 3

V7X_LANES = 128
V7X_SUBLANES = 8
V7X_MXU_DIM = 256
V7X_BF16_ROWS_PER_VREG = 16
V7X_VMEM_LIMIT_BYTES = 60 * 1024 * 1024

ROW_TILE = 512
FF_CHUNK = 1024
Q_TILE = 256
K_TILE = 256
HEADS_PER_STEP = 2
CAST_ROWS = 128

_LOG2_E = math.log2(math.e)
_BF16 = jnp.bfloat16
_F32 = jnp.float32


def _dot(a, b):
    return jnp.dot(a, b, preferred_element_type=_F32)


def _dot_nt(a, b):
    return lax.dot_general(a, b, (((1,), (1,)), ((), ())),
                           preferred_element_type=_F32)


def _rms(x, g):
    ms = jnp.mean(x * x, axis=-1, keepdims=True)
    return x * lax.rsqrt(ms + EPS) * g


def _mlp(x, g, wup_ref, wdn_ref):
    h = _rms(x, g).astype(_BF16)
    acc = x
    d_ff = wup_ref.shape[1]
    for c in range(d_ff // FF_CHUNK):
        sl = slice(c * FF_CHUNK, (c + 1) * FF_CHUNK)
        a = jnp.maximum(_dot(h, wup_ref[:, sl]), 0.0)
        acc = acc + _dot((a * a).astype(_BF16), wdn_ref[sl, :])
    return acc


def _head_rms(y, ones_ref, gain):
    y2 = (y * y).astype(_BF16)
    n = y.shape[1]
    parts = [_dot(y2[:, t:t + V7X_MXU_DIM], ones_ref[...])
             for t in range(0, n, V7X_MXU_DIM)]
    ss = jnp.concatenate(parts, axis=1)
    return y * lax.rsqrt(ss * (1.0 / HEAD_DIM) + EPS) * gain


def _const_spec(shape):
    return pl.BlockSpec(shape, lambda *_: (0,) * len(shape),
                        pipeline_mode=pl.Buffered(1))


def _layer_spec(w, layer):
    _, r, c = w.shape
    return pl.BlockSpec((None, r, c), lambda *_: (layer, 0, 0),
                        pipeline_mode=pl.Buffered(1))


def _cast_jobs(weights, n_steps, step_of):
    in_specs, out_specs, shapes = [], [], []
    for w, layer in weights:
        _, r, c = w.shape
        assert r % (n_steps * V7X_BF16_ROWS_PER_VREG) == 0
        block = (None, r // n_steps, c)
        in_specs.append(pl.BlockSpec(
            block, lambda *ids, layer=layer: (layer, step_of(*ids), 0)))
        out_specs.append(pl.BlockSpec(
            block, lambda *ids: (0, step_of(*ids), 0)))
        shapes.append(jax.ShapeDtypeStruct((1, r, c), _BF16))
    return in_specs, out_specs, shapes


def _run_casts(src_refs, dst_refs):
    for src, dst in zip(src_refs, dst_refs):
        dst[...] = src[...].astype(_BF16)


def _mixer_kernel(*refs, n_cast):
    x_ref, g_ref, win_ref, conv_ref, wout_ref = refs[:5]
    cast_src = refs[5:5 + n_cast]
    o_ref = refs[5 + n_cast]
    cast_dst = refs[6 + n_cast:6 + 2 * n_cast]
    carry_ref, win_bf, wout_bf = refs[6 + 2 * n_cast:]
    d = x_ref.shape[2]
    ts = x_ref.shape[1]

    @pl.when((pl.program_id(0) == 0) & (pl.program_id(1) == 0))
    def _():
        def cast_rows(i, _):
            r = pl.ds(pl.multiple_of(i * CAST_ROWS, CAST_ROWS), CAST_ROWS)
            win_bf[r, :] = win_ref[r, :].astype(_BF16)
            wout_bf[r, :] = wout_ref[r, :].astype(_BF16)
            return 0
        lax.fori_loop(0, d // CAST_ROWS, cast_rows, 0)

    @pl.when(pl.program_id(1) == 0)
    def _():
        carry_ref[...] = jnp.zeros_like(carry_ref)

    _run_casts(cast_src, cast_dst)

    x = x_ref[0]
    h = _rms(x, g_ref[...]).astype(_BF16)
    b_gate = _dot(h, win_bf[:, 0:d])
    c_gate = _dot(h, win_bf[:, d:2 * d])
    v = _dot(h, win_bf[:, 2 * d:3 * d])
    u = c_gate * v

    rows = lax.broadcasted_iota(jnp.int32, u.shape, 0)
    tail = carry_ref[...]
    prev1 = tail[V7X_SUBLANES - 1:V7X_SUBLANES]
    prev2 = tail[V7X_SUBLANES - 2:V7X_SUBLANES - 1]
    u1 = jnp.where(rows == 0, prev1, pltpu.roll(u, 1, 0))
    u2 = jnp.where(rows == 0, prev2,
                   jnp.where(rows == 1, prev1, pltpu.roll(u, 2, 0)))
    w = conv_ref[...]
    y = w[0:1] * u2 + w[1:2] * u1 + w[2:3] * u
    carry_ref[...] = u[ts - V7X_SUBLANES:ts]

    o_ref[0] = x + _dot((b_gate * y).astype(_BF16), wout_bf[...])


def _mixer(x, g, w_in, conv_w, w_out, layer, next_weights):
    b, s, d = x.shape
    ts = ROW_TILE
    assert s % ts == 0 and d % CAST_ROWS == 0
    n_seq = s // ts
    tok = pl.BlockSpec((1, ts, d), lambda i, j: (i, j, 0))
    cast_in, cast_out, cast_shapes = _cast_jobs(next_weights, b * n_seq,
                                                lambda i, j: i * n_seq + j)
    outs = pl.pallas_call(
        functools.partial(_mixer_kernel, n_cast=len(cast_in)),
        grid=(b, n_seq),
        in_specs=[tok, _const_spec((1, d)), _layer_spec(w_in, layer),
                  _const_spec((CONV_WIDTH, d)), _layer_spec(w_out, layer),
                  *cast_in],
        out_specs=[tok, *cast_out],
        out_shape=[jax.ShapeDtypeStruct(x.shape, _F32), *cast_shapes],
        scratch_shapes=[pltpu.VMEM((V7X_SUBLANES, d), _F32),
                        pltpu.VMEM(w_in.shape[1:], _BF16),
                        pltpu.VMEM(w_out.shape[1:], _BF16)],
        compiler_params=pltpu.CompilerParams(
            dimension_semantics=("arbitrary", "arbitrary"),
            vmem_limit_bytes=V7X_VMEM_LIMIT_BYTES),
        name="mixer",
    )(x, g, w_in, conv_w, w_out, *[w for w, _ in next_weights])
    return outs[0], outs[1:]


def _mlp_qkv_kernel(*refs, n_cast):
    (x_ref, gm_ref, wup_ref, wdn_ref, gkv_ref, wkv_ref, gk_ref, gb_ref, wq_ref,
     gq_ref, ones_ref) = refs[:11]
    cast_src = refs[11:11 + n_cast]
    x1_ref, q_ref, k_ref, v_ref = refs[11 + n_cast:15 + n_cast]
    cast_dst = refs[15 + n_cast:]

    _run_casts(cast_src, cast_dst)

    x1 = _mlp(x_ref[...], gm_ref[...], wup_ref, wdn_ref)
    x1_ref[...] = x1

    nk = k_ref.shape[1]
    hkv = _rms(x1, gkv_ref[...]).astype(_BF16)
    k = _dot(hkv, wkv_ref[:, 0:nk])
    k_ref[...] = _head_rms(k, ones_ref, gk_ref[...]).astype(_BF16)
    v_ref[...] = _dot(hkv, wkv_ref[:, nk:]).astype(_BF16)

    hq = _rms(x1, gb_ref[...]).astype(_BF16)
    q = _head_rms(_dot(hq, wq_ref[...]), ones_ref, gq_ref[...])
    q_ref[...] = (q * (HEAD_DIM ** -0.5 * _LOG2_E)).astype(_BF16)


def _mlp_qkv(x, g_mlp, w_up, w_down, g_kv, w_kv, g_k, g_b, w_q, g_q, ones,
             next_weights):
    m, d = x.shape
    d_ff = w_up.shape[2]
    nq = w_q.shape[2]
    nkv = w_kv.shape[2]
    nk = nq
    tm = ROW_TILE
    assert m % tm == 0 and d_ff % FF_CHUNK == 0

    def rows(n):
        return pl.BlockSpec((tm, n), lambda i: (i, 0))

    cast_in, cast_out, cast_shapes = _cast_jobs(next_weights, m // tm,
                                                lambda i: i)
    outs = pl.pallas_call(
        functools.partial(_mlp_qkv_kernel, n_cast=len(cast_in)),
        grid=(m // tm,),
        in_specs=[rows(d), _const_spec((1, d)), _layer_spec(w_up, 0),
                  _layer_spec(w_down, 0), _const_spec((1, d)),
                  _layer_spec(w_kv, 0), _const_spec((1, nk)),
                  _const_spec((1, d)), _layer_spec(w_q, 0),
                  _const_spec((1, nq)),
                  _const_spec((V7X_MXU_DIM, V7X_MXU_DIM)), *cast_in],
        out_specs=[rows(d), rows(nq), rows(nk), rows(nkv - nk), *cast_out],
        out_shape=[jax.ShapeDtypeStruct((m, d), _F32),
                   jax.ShapeDtypeStruct((m, nq), _BF16),
                   jax.ShapeDtypeStruct((m, nk), _BF16),
                   jax.ShapeDtypeStruct((m, nkv - nk), _BF16), *cast_shapes],
        compiler_params=pltpu.CompilerParams(
            dimension_semantics=("arbitrary",),
            vmem_limit_bytes=V7X_VMEM_LIMIT_BYTES),
        name="mlp_qkv",
    )(x, g_mlp, w_up, w_down, g_kv, w_kv, g_k, g_b, w_q, g_q, ones,
      *[w for w, _ in next_weights])
    return outs[:4], outs[4:]


def _attn_kernel(q_ref, k_ref, v_ref, lq1_ref, lk1_ref, lq2_ref, lk2_ref,
                 sub_ref, o_ref, s_scr, p_scr, vx_scr, *, lam_init):
    s_len = q_ref.shape[1]
    n_heads = q_ref.shape[2] // V_DIM
    tq, tk = Q_TILE, K_TILE
    lam = (jnp.exp(jnp.sum(lq1_ref[...] * lk1_ref[...], keepdims=True))
           - jnp.exp(jnp.sum(lq2_ref[...] * lk2_ref[...], keepdims=True))
           + lam_init)
    out_gain = sub_ref[...] * (1.0 - lam_init)

    for h in range(n_heads):
        vx_scr[h, :, 0:V_DIM] = v_ref[0, :, h * V_DIM:(h + 1) * V_DIM]
        vx_scr[h, :, V_DIM:2 * V_DIM] = jnp.ones((s_len, V_DIM), _BF16)

    lane = lax.broadcasted_iota(jnp.int32, (tq, V_DIM), 1)
    row_id = lax.broadcasted_iota(jnp.int32, (tq, tk), 0)
    row_id = jnp.concatenate([row_id, row_id], axis=0)
    col_id = lax.broadcasted_iota(jnp.int32, (2 * tq, tk), 1)

    def lane_tiles(a):
        return [a[:, t:t + V7X_LANES] for t in range(0, a.shape[1], V7X_LANES)]

    tiles = [(h, qi) for h in range(n_heads)
             for qi in reversed(range(s_len // tq))]
    row_max = {}

    def scores(pos):
        (h, qi), slot = tiles[pos], pos % 2
        q_lo, cols = qi * tq, slice(h * V_DIM, (h + 1) * V_DIM)
        q = q_ref[0, q_lo:q_lo + tq, cols]
        zero = jnp.zeros_like(q)
        q2 = jnp.concatenate([jnp.where(lane < HEAD_DIM, q, zero),
                              jnp.where(lane >= HEAD_DIM, q, zero)], axis=0)
        m_run = jnp.full((2 * tq, V7X_LANES), NEG_INF, _F32)
        for k_lo in range(0, q_lo + tq, tk):
            s = _dot_nt(q2, k_ref[0, k_lo:k_lo + tk, cols])
            if k_lo + tk > q_lo:
                visible = (k_lo + col_id) <= (q_lo + row_id)
                s = jnp.where(visible, s, NEG_INF)
            s_scr[slot, :, k_lo:k_lo + tk] = s
            for part in lane_tiles(s):
                m_run = jnp.maximum(m_run, part)
            yield
        row_max[pos] = jnp.max(m_run, axis=-1, keepdims=True)

    def probs(pos):
        (_, qi), slot = tiles[pos], pos % 2
        for k_lo in range(0, qi * tq + tq, tk):
            p = jnp.exp2(s_scr[slot, :, k_lo:k_lo + tk] - row_max[pos])
            p_scr[slot, :, k_lo:k_lo + tk] = p.astype(_BF16)
            yield

    def values(pos):
        (h, qi), slot = tiles[pos], pos % 2
        q_lo, n_keys = qi * tq, qi * tq + tq
        pv = _dot(p_scr[slot, :, 0:n_keys], vx_scr[h, 0:n_keys, :])
        maps = pv[:, 0:V_DIM] / pv[:, V_DIM:2 * V_DIM]
        o = maps[0:tq] - lam * maps[tq:2 * tq]
        o_ref[0, q_lo:q_lo + tq, h * V_DIM:(h + 1) * V_DIM] = (
            _rms(o, out_gain).astype(_BF16))

    for _ in scores(0):
        pass
    for pos in range(len(tiles) + 1):
        if pos >= 1:
            values(pos - 1)
        active = [stage(p) for stage, p in ((probs, pos), (scores, pos + 1))
                  if p < len(tiles)]
        while active:
            for gen in list(active):
                if next(gen, StopIteration) is StopIteration:
                    active.remove(gen)


def _attention(q, k, v, lq1, lk1, lq2, lk2, sub, lam_init):
    b, s, n = q.shape
    n_heads = n // V_DIM
    assert s % Q_TILE == 0 and s % K_TILE == 0 and n_heads % HEADS_PER_STEP == 0
    head = pl.BlockSpec((1, s, HEADS_PER_STEP * V_DIM), lambda i, h: (i, 0, h))
    vec = _const_spec((1, HEAD_DIM))
    return pl.pallas_call(
        functools.partial(_attn_kernel, lam_init=lam_init),
        grid=(b, n_heads // HEADS_PER_STEP),
        in_specs=[head, head, head, vec, vec, vec, vec, _const_spec((1, V_DIM))],
        out_specs=head,
        out_shape=jax.ShapeDtypeStruct((b, s, n), _BF16),
        scratch_shapes=[pltpu.VMEM((2, 2 * Q_TILE, s), _F32),
                        pltpu.VMEM((2, 2 * Q_TILE, s), _BF16),
                        pltpu.VMEM((HEADS_PER_STEP, s, 2 * V_DIM), _BF16)],
        compiler_params=pltpu.CompilerParams(
            dimension_semantics=("arbitrary", "arbitrary"),
            vmem_limit_bytes=V7X_VMEM_LIMIT_BYTES),
        name="attention",
    )(q, k, v, lq1, lk1, lq2, lk2, sub)


def _out_mlp_kernel(x_ref, o_ref, wo_ref, gm_ref, wup_ref, wdn_ref, y_ref):
    x2 = x_ref[...] + _dot(o_ref[...], wo_ref[...])
    y_ref[...] = _mlp(x2, gm_ref[...], wup_ref, wdn_ref)


def _out_mlp(x, o, w_o, g_mlp, w_up, w_down):
    m, d = x.shape
    n = o.shape[1]
    d_ff = w_up.shape[2]
    tm = ROW_TILE
    assert m % tm == 0 and d_ff % FF_CHUNK == 0
    return pl.pallas_call(
        _out_mlp_kernel,
        grid=(m // tm,),
        in_specs=[pl.BlockSpec((tm, d), lambda i: (i, 0)),
                  pl.BlockSpec((tm, n), lambda i: (i, 0)),
                  _layer_spec(w_o, 0), _const_spec((1, d)),
                  _layer_spec(w_up, 0), _layer_spec(w_down, 0)],
        out_specs=pl.BlockSpec((tm, d), lambda i: (i, 0)),
        out_shape=jax.ShapeDtypeStruct((m, d), _F32),
        compiler_params=pltpu.CompilerParams(
            dimension_semantics=("arbitrary",),
            vmem_limit_bytes=V7X_VMEM_LIMIT_BYTES),
        name="out_mlp",
    )(x, o, w_o, g_mlp, w_up, w_down)


def _lambda_init(layer_idx):
    return 0.8 - 0.6 * math.exp(-0.3 * layer_idx)


def kernel(x, a_norm, a_w_in, a_conv, a_w_out, kv_norm, w_kv, k_norm, b_norm,
           w_q, q_norm, lam_q1, lam_k1, lam_q2, lam_k2, sub_norm, w_o,
           mlp_norm, w_up, w_down):
    b, s, d = x.shape
    n_a, n_b = a_norm.shape[0], b_norm.shape[0]
    assert n_a == 1 and n_b == 1, "one mixer layer followed by one attention layer"
    n_qk = w_q.shape[2]
    assert w_kv.shape[1] == 2 * n_qk and n_qk % V7X_MXU_DIM == 0

    row = lambda g: g.reshape(1, -1)
    groups = n_qk // HEAD_DIM
    gid = jnp.arange(V7X_MXU_DIM) // HEAD_DIM
    ones = (gid[:, None] == gid[None, :]).astype(_BF16)

    x, (w_up_bf0, w_down_bf0, w_kv_bf, w_q_bf) = _mixer(
        x, row(a_norm[0]), a_w_in, a_conv[0], a_w_out, 0,
        [(w_up, 0), (w_down, 0), (w_kv[None], 0), (w_q, 0)])
    (x1, q, k, v), (w_o_bf, w_up_bf1, w_down_bf1) = _mlp_qkv(
        x.reshape(b * s, d), row(mlp_norm[0]), w_up_bf0, w_down_bf0,
        row(kv_norm), w_kv_bf, row(jnp.tile(k_norm, groups)),
        row(b_norm[0]), w_q_bf, row(jnp.tile(q_norm[0], groups)), ones,
        [(w_o, 0), (w_up, 1), (w_down, 1)])
    o = _attention(q.reshape(b, s, n_qk), k.reshape(b, s, n_qk),
                   v.reshape(b, s, -1), row(lam_q1[0]), row(lam_k1[0]),
                   row(lam_q2[0]), row(lam_k2[0]), row(sub_norm[0]),
                   _lambda_init(n_a))
    y = _out_mlp(x1, o.reshape(b * s, -1), w_o_bf, row(mlp_norm[1]),
                 w_up_bf1, w_down_bf1)
    return y.reshape(b, s, d)
```

```python
import functools
import math

import jax
import jax.numpy as jnp
from jax import lax
from jax.experimental import pallas as pl
from jax.experimental.pallas import tpu as pltpu

EPS = 1e-6
NEG_INF = -1e30
HEAD_DIM = 64
V_DIM = 2 * HEAD_DIM
CONV_WIDTH = 3

V7X_LANES = 128
V7X_SUBLANES = 8
V7X_MXU_DIM = 256
V7X_BF16_ROWS_PER_VREG = 16
V7X_VMEM_LIMIT_BYTES = 60 * 1024 * 1024

ROW_TILE = 512
OUT_ROW_TILE = 1024
FF_CHUNK = 1024
Q_TILE = 256
K_TILE = 256
HEADS_PER_STEP = 2
CAST_ROWS = 128

_LOG2_E = math.log2(math.e)
_BF16 = jnp.bfloat16
_F32 = jnp.float32


def _dot(a, b):
    return jnp.dot(a, b, preferred_element_type=_F32)


def _dot_nt(a, b):
    return lax.dot_general(a, b, (((1,), (1,)), ((), ())),
                           preferred_element_type=_F32)


def _rms(x, g):
    ms = jnp.mean(x * x, axis=-1, keepdims=True)
    return x * lax.rsqrt(ms + EPS) * g


def _mlp(x, g, wup_ref, wdn_ref):
    h = _rms(x, g).astype(_BF16)
    acc = x
    d_ff = wup_ref.shape[1]
    for c in range(d_ff // FF_CHUNK):
        sl = slice(c * FF_CHUNK, (c + 1) * FF_CHUNK)
        a = jnp.maximum(_dot(h, wup_ref[:, sl]), 0.0)
        acc = acc + _dot((a * a).astype(_BF16), wdn_ref[sl, :])
    return acc


def _head_rms(y, ones_ref, gain):
    y2 = (y * y).astype(_BF16)
    n = y.shape[1]
    parts = [_dot(y2[:, t:t + V7X_MXU_DIM], ones_ref[...])
             for t in range(0, n, V7X_MXU_DIM)]
    ss = jnp.concatenate(parts, axis=1)
    return y * lax.rsqrt(ss * (1.0 / HEAD_DIM) + EPS) * gain


def _const_spec(shape):
    return pl.BlockSpec(shape, lambda *_: (0,) * len(shape),
                        pipeline_mode=pl.Buffered(1))


def _layer_spec(w, layer):
    _, r, c = w.shape
    return pl.BlockSpec((None, r, c), lambda *_: (layer, 0, 0),
                        pipeline_mode=pl.Buffered(1))


def _cast_jobs(weights, n_steps, step_of):
    in_specs, out_specs, shapes = [], [], []
    for w, layer in weights:
        _, r, c = w.shape
        assert r % (n_steps * V7X_BF16_ROWS_PER_VREG) == 0
        block = (None, r // n_steps, c)
        in_specs.append(pl.BlockSpec(
            block, lambda *ids, layer=layer: (layer, step_of(*ids), 0)))
        out_specs.append(pl.BlockSpec(
            block, lambda *ids: (0, step_of(*ids), 0)))
        shapes.append(jax.ShapeDtypeStruct((1, r, c), _BF16))
    return in_specs, out_specs, shapes


def _run_casts(src_refs, dst_refs):
    for src, dst in zip(src_refs, dst_refs):
        dst[...] = src[...].astype(_BF16)


def _mixer_kernel(*refs, n_cast):
    x_ref, g_ref, win_ref, conv_ref, wout_ref = refs[:5]
    cast_src = refs[5:5 + n_cast]
    o_ref = refs[5 + n_cast]
    cast_dst = refs[6 + n_cast:6 + 2 * n_cast]
    carry_ref, win_bf, wout_bf = refs[6 + 2 * n_cast:]
    d = x_ref.shape[2]
    ts = x_ref.shape[1]

    @pl.when((pl.program_id(0) == 0) & (pl.program_id(1) == 0))
    def _():
        def cast_rows(i, _):
            r = pl.ds(pl.multiple_of(i * CAST_ROWS, CAST_ROWS), CAST_ROWS)
            win_bf[r, :] = win_ref[r, :].astype(_BF16)
            wout_bf[r, :] = wout_ref[r, :].astype(_BF16)
            return 0
        lax.fori_loop(0, d // CAST_ROWS, cast_rows, 0)

    @pl.when(pl.program_id(1) == 0)
    def _():
        carry_ref[...] = jnp.zeros_like(carry_ref)

    _run_casts(cast_src, cast_dst)

    x = x_ref[0]
    h = _rms(x, g_ref[...]).astype(_BF16)
    b_gate = _dot(h, win_bf[:, 0:d])
    c_gate = _dot(h, win_bf[:, d:2 * d])
    v = _dot(h, win_bf[:, 2 * d:3 * d])
    u = c_gate * v

    rows = lax.broadcasted_iota(jnp.int32, u.shape, 0)
    tail = carry_ref[...]
    prev1 = tail[V7X_SUBLANES - 1:V7X_SUBLANES]
    prev2 = tail[V7X_SUBLANES - 2:V7X_SUBLANES - 1]
    u1 = jnp.where(rows == 0, prev1, pltpu.roll(u, 1, 0))
    u2 = jnp.where(rows == 0, prev2,
                   jnp.where(rows == 1, prev1, pltpu.roll(u, 2, 0)))
    w = conv_ref[...]
    y = w[0:1] * u2 + w[1:2] * u1 + w[2:3] * u
    carry_ref[...] = u[ts - V7X_SUBLANES:ts]

    o_ref[0] = x + _dot((b_gate * y).astype(_BF16), wout_bf[...])


def _mixer(x, g, w_in, conv_w, w_out, layer, next_weights):
    b, s, d = x.shape
    ts = ROW_TILE
    assert s % ts == 0 and d % CAST_ROWS == 0
    n_seq = s // ts
    tok = pl.BlockSpec((1, ts, d), lambda i, j: (i, j, 0))
    cast_in, cast_out, cast_shapes = _cast_jobs(next_weights, b * n_seq,
                                                lambda i, j: i * n_seq + j)
    outs = pl.pallas_call(
        functools.partial(_mixer_kernel, n_cast=len(cast_in)),
        grid=(b, n_seq),
        in_specs=[tok, _const_spec((1, d)), _layer_spec(w_in, layer),
                  _const_spec((CONV_WIDTH, d)), _layer_spec(w_out, layer),
                  *cast_in],
        out_specs=[tok, *cast_out],
        out_shape=[jax.ShapeDtypeStruct(x.shape, _F32), *cast_shapes],
        scratch_shapes=[pltpu.VMEM((V7X_SUBLANES, d), _F32),
                        pltpu.VMEM(w_in.shape[1:], _BF16),
                        pltpu.VMEM(w_out.shape[1:], _BF16)],
        compiler_params=pltpu.CompilerParams(
            dimension_semantics=("arbitrary", "arbitrary"),
            vmem_limit_bytes=V7X_VMEM_LIMIT_BYTES),
        name="mixer",
    )(x, g, w_in, conv_w, w_out, *[w for w, _ in next_weights])
    return outs[0], outs[1:]


def _mlp_qkv_kernel(*refs, n_cast):
    (x_ref, gm_ref, wup_ref, wdn_ref, gkv_ref, wkv_ref, gk_ref, gb_ref, wq_ref,
     gq_ref, ones_ref) = refs[:11]
    cast_src = refs[11:11 + n_cast]
    x1_ref, q_ref, k_ref, v_ref = refs[11 + n_cast:15 + n_cast]
    cast_dst = refs[15 + n_cast:]

    _run_casts(cast_src, cast_dst)

    x1 = _mlp(x_ref[...], gm_ref[...], wup_ref, wdn_ref)
    x1_ref[...] = x1

    nk = k_ref.shape[1]
    hkv = _rms(x1, gkv_ref[...]).astype(_BF16)
    k = _dot(hkv, wkv_ref[:, 0:nk])
    k_ref[...] = _head_rms(k, ones_ref, gk_ref[...]).astype(_BF16)
    v_ref[...] = _dot(hkv, wkv_ref[:, nk:]).astype(_BF16)

    hq = _rms(x1, gb_ref[...]).astype(_BF16)
    q = _head_rms(_dot(hq, wq_ref[...]), ones_ref, gq_ref[...])
    q_ref[...] = (q * (HEAD_DIM ** -0.5 * _LOG2_E)).astype(_BF16)


def _mlp_qkv(x, g_mlp, w_up, w_down, g_kv, w_kv, g_k, g_b, w_q, g_q, ones,
             next_weights):
    m, d = x.shape
    d_ff = w_up.shape[2]
    nq = w_q.shape[2]
    nkv = w_kv.shape[2]
    nk = nq
    tm = ROW_TILE
    assert m % tm == 0 and d_ff % FF_CHUNK == 0

    def rows(n):
        return pl.BlockSpec((tm, n), lambda i: (i, 0))

    cast_in, cast_out, cast_shapes = _cast_jobs(next_weights, m // tm,
                                                lambda i: i)
    outs = pl.pallas_call(
        functools.partial(_mlp_qkv_kernel, n_cast=len(cast_in)),
        grid=(m // tm,),
        in_specs=[rows(d), _const_spec((1, d)), _layer_spec(w_up, 0),
                  _layer_spec(w_down, 0), _const_spec((1, d)),
                  _layer_spec(w_kv, 0), _const_spec((1, nk)),
                  _const_spec((1, d)), _layer_spec(w_q, 0),
                  _const_spec((1, nq)),
                  _const_spec((V7X_MXU_DIM, V7X_MXU_DIM)), *cast_in],
        out_specs=[rows(d), rows(nq), rows(nk), rows(nkv - nk), *cast_out],
        out_shape=[jax.ShapeDtypeStruct((m, d), _F32),
                   jax.ShapeDtypeStruct((m, nq), _BF16),
                   jax.ShapeDtypeStruct((m, nk), _BF16),
                   jax.ShapeDtypeStruct((m, nkv - nk), _BF16), *cast_shapes],
        compiler_params=pltpu.CompilerParams(
            dimension_semantics=("arbitrary",),
            vmem_limit_bytes=V7X_VMEM_LIMIT_BYTES),
        name="mlp_qkv",
    )(x, g_mlp, w_up, w_down, g_kv, w_kv, g_k, g_b, w_q, g_q, ones,
      *[w for w, _ in next_weights])
    return outs[:4], outs[4:]


def _attn_kernel(q_ref, k_ref, v_ref, lq1_ref, lk1_ref, lq2_ref, lk2_ref,
                 sub_ref, o_ref, s_scr, p_scr, vx_scr, *, lam_init):
    s_len = q_ref.shape[1]
    n_heads = q_ref.shape[2] // V_DIM
    tq, tk = Q_TILE, K_TILE
    lam = (jnp.exp(jnp.sum(lq1_ref[...] * lk1_ref[...], keepdims=True))
           - jnp.exp(jnp.sum(lq2_ref[...] * lk2_ref[...], keepdims=True))
           + lam_init)
    out_gain = sub_ref[...] * (1.0 - lam_init)

    for h in range(n_heads):
        vx_scr[h, :, 0:V_DIM] = v_ref[0, :, h * V_DIM:(h + 1) * V_DIM]
        vx_scr[h, :, V_DIM:2 * V_DIM] = jnp.ones((s_len, V_DIM), _BF16)

    lane = lax.broadcasted_iota(jnp.int32, (tq, V_DIM), 1)
    row_id = lax.broadcasted_iota(jnp.int32, (tq, tk), 0)
    row_id = jnp.concatenate([row_id, row_id], axis=0)
    col_id = lax.broadcasted_iota(jnp.int32, (2 * tq, tk), 1)

    def lane_tiles(a):
        return [a[:, t:t + V7X_LANES] for t in range(0, a.shape[1], V7X_LANES)]

    tiles = [(h, qi) for h in range(n_heads)
             for qi in reversed(range(s_len // tq))]
    row_max = {}

    def scores(pos):
        (h, qi), slot = tiles[pos], pos % 2
        q_lo, cols = qi * tq, slice(h * V_DIM, (h + 1) * V_DIM)
        q = q_ref[0, q_lo:q_lo + tq, cols]
        zero = jnp.zeros_like(q)
        q2 = jnp.concatenate([jnp.where(lane < HEAD_DIM, q, zero),
                              jnp.where(lane >= HEAD_DIM, q, zero)], axis=0)
        m_run = jnp.full((2 * tq, V7X_LANES), NEG_INF, _F32)
        for k_lo in range(0, q_lo + tq, tk):
            s = _dot_nt(q2, k_ref[0, k_lo:k_lo + tk, cols])
            if k_lo + tk > q_lo:
                visible = (k_lo + col_id) <= (q_lo + row_id)
                s = jnp.where(visible, s, NEG_INF)
            s_scr[slot, :, k_lo:k_lo + tk] = s
            for part in lane_tiles(s):
                m_run = jnp.maximum(m_run, part)
            yield
        row_max[pos] = jnp.max(m_run, axis=-1, keepdims=True)

    def probs(pos):
        (_, qi), slot = tiles[pos], pos % 2
        for k_lo in range(0, qi * tq + tq, tk):
            p = jnp.exp2(s_scr[slot, :, k_lo:k_lo + tk] - row_max[pos])
            p_scr[slot, :, k_lo:k_lo + tk] = p.astype(_BF16)
            yield

    def values(pos):
        (h, qi), slot = tiles[pos], pos % 2
        q_lo, n_keys = qi * tq, qi * tq + tq
        pv = _dot(p_scr[slot, :, 0:n_keys], vx_scr[h, 0:n_keys, :])
        maps = pv[:, 0:V_DIM] / pv[:, V_DIM:2 * V_DIM]
        o = maps[0:tq] - lam * maps[tq:2 * tq]
        o_ref[0, q_lo:q_lo + tq, h * V_DIM:(h + 1) * V_DIM] = (
            _rms(o, out_gain).astype(_BF16))

    for _ in scores(0):
        pass
    for pos in range(len(tiles) + 1):
        if pos >= 1:
            values(pos - 1)
        active = [stage(p) for stage, p in ((probs, pos), (scores, pos + 1))
                  if p < len(tiles)]
        while active:
            for gen in list(active):
                if next(gen, StopIteration) is StopIteration:
                    active.remove(gen)


def _attention(q, k, v, lq1, lk1, lq2, lk2, sub, lam_init):
    b, s, n = q.shape
    n_heads = n // V_DIM
    assert s % Q_TILE == 0 and s % K_TILE == 0 and n_heads % HEADS_PER_STEP == 0
    head = pl.BlockSpec((1, s, HEADS_PER_STEP * V_DIM), lambda i, h: (i, 0, h))
    vec = _const_spec((1, HEAD_DIM))
    return pl.pallas_call(
        functools.partial(_attn_kernel, lam_init=lam_init),
        grid=(b, n_heads // HEADS_PER_STEP),
        in_specs=[head, head, head, vec, vec, vec, vec, _const_spec((1, V_DIM))],
        out_specs=head,
        out_shape=jax.ShapeDtypeStruct((b, s, n), _BF16),
        scratch_shapes=[pltpu.VMEM((2, 2 * Q_TILE, s), _F32),
                        pltpu.VMEM((2, 2 * Q_TILE, s), _BF16),
                        pltpu.VMEM((HEADS_PER_STEP, s, 2 * V_DIM), _BF16)],
        compiler_params=pltpu.CompilerParams(
            dimension_semantics=("arbitrary", "arbitrary"),
            vmem_limit_bytes=V7X_VMEM_LIMIT_BYTES),
        name="attention",
    )(q, k, v, lq1, lk1, lq2, lk2, sub)


def _out_mlp_kernel(x_ref, o_ref, wo_ref, gm_ref, wup_ref, wdn_ref, y_ref):
    x2 = x_ref[...] + _dot(o_ref[...], wo_ref[...])
    y_ref[...] = _mlp(x2, gm_ref[...], wup_ref, wdn_ref)


def _out_mlp(x, o, w_o, g_mlp, w_up, w_down):
    m, d = x.shape
    n = o.shape[1]
    d_ff = w_up.shape[2]
    tm = OUT_ROW_TILE
    assert m % tm == 0 and d_ff % FF_CHUNK == 0
    return pl.pallas_call(
        _out_mlp_kernel,
        grid=(m // tm,),
        in_specs=[pl.BlockSpec((tm, d), lambda i: (i, 0)),
                  pl.BlockSpec((tm, n), lambda i: (i, 0)),
                  _layer_spec(w_o, 0), _const_spec((1, d)),
                  _layer_spec(w_up, 0), _layer_spec(w_down, 0)],
        out_specs=pl.BlockSpec((tm, d), lambda i: (i, 0)),
        out_shape=jax.ShapeDtypeStruct((m, d), _F32),
        compiler_params=pltpu.CompilerParams(
            dimension_semantics=("arbitrary",),
            vmem_limit_bytes=V7X_VMEM_LIMIT_BYTES),
        name="out_mlp",
    )(x, o, w_o, g_mlp, w_up, w_down)


def _lambda_init(layer_idx):
    return 0.8 - 0.6 * math.exp(-0.3 * layer_idx)


def kernel(x, a_norm, a_w_in, a_conv, a_w_out, kv_norm, w_kv, k_norm, b_norm,
           w_q, q_norm, lam_q1, lam_k1, lam_q2, lam_k2, sub_norm, w_o,
           mlp_norm, w_up, w_down):
    b, s, d = x.shape
    n_a, n_b = a_norm.shape[0], b_norm.shape[0]
    assert n_a == 1 and n_b == 1, "one mixer layer followed by one attention layer"
    n_qk = w_q.shape[2]
    assert w_kv.shape[1] == 2 * n_qk and n_qk % V7X_MXU_DIM == 0

    row = lambda g: g.reshape(1, -1)
    groups = n_qk // HEAD_DIM
    gid = jnp.arange(V7X_MXU_DIM) // HEAD_DIM
    ones = (gid[:, None] == gid[None, :]).astype(_BF16)

    x, (w_up_bf0, w_down_bf0, w_kv_bf, w_q_bf) = _mixer(
        x, row(a_norm[0]), a_w_in, a_conv[0], a_w_out, 0,
        [(w_up, 0), (w_down, 0), (w_kv[None], 0), (w_q, 0)])
    (x1, q, k, v), (w_o_bf, w_up_bf1, w_down_bf1) = _mlp_qkv(
        x.reshape(b * s, d), row(mlp_norm[0]), w_up_bf0, w_down_bf0,
        row(kv_norm), w_kv_bf, row(jnp.tile(k_norm, groups)),
        row(b_norm[0]), w_q_bf, row(jnp.tile(q_norm[0], groups)), ones,
        [(w_o, 0), (w_up, 1), (w_down, 1)])
    o = _attention(q.reshape(b, s, n_qk), k.reshape(b, s, n_qk),
                   v.reshape(b, s, -1), row(lam_q1[0]), row(lam_k1[0]),
                   row(lam_q2[0]), row(lam_k2[0]), row(sub_norm[0]),
                   _lambda_init(n_a))
    y = _out_mlp(x1, o.reshape(b * s, -1), w_o_bf, row(mlp_norm[1]),
                 w_up_bf1, w_down_bf1)
    return y.reshape(b, s, d)
```

```python
import functools
import math

import jax
import jax.numpy as jnp
from jax import lax
from jax.experimental import pallas as pl
from jax.experimental.pallas import tpu as pltpu

EPS = 1e-6
NEG_INF = -1e30
HEAD_DIM = 64
V_DIM = 2 * HEAD_DIM
CONV_WIDTH = 3

V7X_LANES = 128
V7X_SUBLANES = 8
V7X_MXU_DIM = 256
V7X_BF16_ROWS_PER_VREG = 16
V7X_VMEM_LIMIT_BYTES = 60 * 1024 * 1024

ROW_TILE = 512
BIG_ROW_TILE = 1024
FF_CHUNK = 1024
Q_TILE = 256
K_TILE = 256
HEADS_PER_STEP = 2
CAST_ROWS = 128

_LOG2_E = math.log2(math.e)
_BF16 = jnp.bfloat16
_F32 = jnp.float32


def _dot(a, b):
    return jnp.dot(a, b, preferred_element_type=_F32)


def _dot_nt(a, b):
    return lax.dot_general(a, b, (((1,), (1,)), ((), ())),
                           preferred_element_type=_F32)


def _rms(x, g):
    ms = jnp.mean(x * x, axis=-1, keepdims=True)
    return x * lax.rsqrt(ms + EPS) * g


def _mlp(x, g, wup_ref, wdn_ref):
    h = _rms(x, g).astype(_BF16)
    acc = x
    d_ff = wup_ref.shape[1]
    for c in range(d_ff // FF_CHUNK):
        sl = slice(c * FF_CHUNK, (c + 1) * FF_CHUNK)
        a = jnp.maximum(_dot(h, wup_ref[:, sl]), 0.0)
        acc = acc + _dot((a * a).astype(_BF16), wdn_ref[sl, :])
    return acc


def _head_rms(y, ones_ref, gain):
    y2 = (y * y).astype(_BF16)
    n = y.shape[1]
    parts = [_dot(y2[:, t:t + V7X_MXU_DIM], ones_ref[...])
             for t in range(0, n, V7X_MXU_DIM)]
    ss = jnp.concatenate(parts, axis=1)
    return y * lax.rsqrt(ss * (1.0 / HEAD_DIM) + EPS) * gain


def _const_spec(shape):
    return pl.BlockSpec(shape, lambda *_: (0,) * len(shape),
                        pipeline_mode=pl.Buffered(1))


def _layer_spec(w, layer):
    _, r, c = w.shape
    return pl.BlockSpec((None, r, c), lambda *_: (layer, 0, 0),
                        pipeline_mode=pl.Buffered(1))


def _cast_jobs(weights, n_steps, step_of):
    in_specs, out_specs, shapes = [], [], []
    for w, layer in weights:
        _, r, c = w.shape
        assert r % (n_steps * V7X_BF16_ROWS_PER_VREG) == 0
        block = (None, r // n_steps, c)
        in_specs.append(pl.BlockSpec(
            block, lambda *ids, layer=layer: (layer, step_of(*ids), 0)))
        out_specs.append(pl.BlockSpec(
            block, lambda *ids: (0, step_of(*ids), 0)))
        shapes.append(jax.ShapeDtypeStruct((1, r, c), _BF16))
    return in_specs, out_specs, shapes


def _run_casts(src_refs, dst_refs):
    for src, dst in zip(src_refs, dst_refs):
        dst[...] = src[...].astype(_BF16)


def _mixer_kernel(*refs, n_cast):
    x_ref, g_ref, win_ref, conv_ref, wout_ref = refs[:5]
    cast_src = refs[5:5 + n_cast]
    o_ref = refs[5 + n_cast]
    cast_dst = refs[6 + n_cast:6 + 2 * n_cast]
    carry_ref, win_bf, wout_bf = refs[6 + 2 * n_cast:]
    d = x_ref.shape[2]
    ts = x_ref.shape[1]

    @pl.when((pl.program_id(0) == 0) & (pl.program_id(1) == 0))
    def _():
        def cast_rows(i, _):
            r = pl.ds(pl.multiple_of(i * CAST_ROWS, CAST_ROWS), CAST_ROWS)
            win_bf[r, :] = win_ref[r, :].astype(_BF16)
            wout_bf[r, :] = wout_ref[r, :].astype(_BF16)
            return 0
        lax.fori_loop(0, d // CAST_ROWS, cast_rows, 0)

    @pl.when(pl.program_id(1) == 0)
    def _():
        carry_ref[...] = jnp.zeros_like(carry_ref)

    _run_casts(cast_src, cast_dst)

    x = x_ref[0]
    h = _rms(x, g_ref[...]).astype(_BF16)
    b_gate = _dot(h, win_bf[:, 0:d])
    c_gate = _dot(h, win_bf[:, d:2 * d])
    v = _dot(h, win_bf[:, 2 * d:3 * d])
    u = c_gate * v

    rows = lax.broadcasted_iota(jnp.int32, u.shape, 0)
    tail = carry_ref[...]
    prev1 = tail[V7X_SUBLANES - 1:V7X_SUBLANES]
    prev2 = tail[V7X_SUBLANES - 2:V7X_SUBLANES - 1]
    u1 = jnp.where(rows == 0, prev1, pltpu.roll(u, 1, 0))
    u2 = jnp.where(rows == 0, prev2,
                   jnp.where(rows == 1, prev1, pltpu.roll(u, 2, 0)))
    w = conv_ref[...]
    y = w[0:1] * u2 + w[1:2] * u1 + w[2:3] * u
    carry_ref[...] = u[ts - V7X_SUBLANES:ts]

    o_ref[0] = x + _dot((b_gate * y).astype(_BF16), wout_bf[...])


def _mixer(x, g, w_in, conv_w, w_out, layer, next_weights):
    b, s, d = x.shape
    ts = BIG_ROW_TILE
    assert s % ts == 0 and d % CAST_ROWS == 0
    n_seq = s // ts
    tok = pl.BlockSpec((1, ts, d), lambda i, j: (i, j, 0))
    cast_in, cast_out, cast_shapes = _cast_jobs(next_weights, b * n_seq,
                                                lambda i, j: i * n_seq + j)
    outs = pl.pallas_call(
        functools.partial(_mixer_kernel, n_cast=len(cast_in)),
        grid=(b, n_seq),
        in_specs=[tok, _const_spec((1, d)), _layer_spec(w_in, layer),
                  _const_spec((CONV_WIDTH, d)), _layer_spec(w_out, layer),
                  *cast_in],
        out_specs=[tok, *cast_out],
        out_shape=[jax.ShapeDtypeStruct(x.shape, _F32), *cast_shapes],
        scratch_shapes=[pltpu.VMEM((V7X_SUBLANES, d), _F32),
                        pltpu.VMEM(w_in.shape[1:], _BF16),
                        pltpu.VMEM(w_out.shape[1:], _BF16)],
        compiler_params=pltpu.CompilerParams(
            dimension_semantics=("arbitrary", "arbitrary"),
            vmem_limit_bytes=V7X_VMEM_LIMIT_BYTES),
        name="mixer",
    )(x, g, w_in, conv_w, w_out, *[w for w, _ in next_weights])
    return outs[0], outs[1:]


def _mlp_qkv_kernel(*refs, n_cast):
    (x_ref, gm_ref, wup_ref, wdn_ref, gkv_ref, wkv_ref, gk_ref, gb_ref, wq_ref,
     gq_ref, ones_ref) = refs[:11]
    cast_src = refs[11:11 + n_cast]
    x1_ref, q_ref, k_ref, v_ref = refs[11 + n_cast:15 + n_cast]
    cast_dst = refs[15 + n_cast:]

    _run_casts(cast_src, cast_dst)

    x1 = _mlp(x_ref[...], gm_ref[...], wup_ref, wdn_ref)
    x1_ref[...] = x1

    nk = k_ref.shape[1]
    hkv = _rms(x1, gkv_ref[...]).astype(_BF16)
    k = _dot(hkv, wkv_ref[:, 0:nk])
    k_ref[...] = _head_rms(k, ones_ref, gk_ref[...]).astype(_BF16)
    v_ref[...] = _dot(hkv, wkv_ref[:, nk:]).astype(_BF16)

    hq = _rms(x1, gb_ref[...]).astype(_BF16)
    q = _head_rms(_dot(hq, wq_ref[...]), ones_ref, gq_ref[...])
    q_ref[...] = (q * (HEAD_DIM ** -0.5 * _LOG2_E)).astype(_BF16)


def _mlp_qkv(x, g_mlp, w_up, w_down, g_kv, w_kv, g_k, g_b, w_q, g_q, ones,
             next_weights):
    m, d = x.shape
    d_ff = w_up.shape[2]
    nq = w_q.shape[2]
    nkv = w_kv.shape[2]
    nk = nq
    tm = ROW_TILE
    assert m % tm == 0 and d_ff % FF_CHUNK == 0

    def rows(n):
        return pl.BlockSpec((tm, n), lambda i: (i, 0))

    cast_in, cast_out, cast_shapes = _cast_jobs(next_weights, m // tm,
                                                lambda i: i)
    outs = pl.pallas_call(
        functools.partial(_mlp_qkv_kernel, n_cast=len(cast_in)),
        grid=(m // tm,),
        in_specs=[rows(d), _const_spec((1, d)), _layer_spec(w_up, 0),
                  _layer_spec(w_down, 0), _const_spec((1, d)),
                  _layer_spec(w_kv, 0), _const_spec((1, nk)),
                  _const_spec((1, d)), _layer_spec(w_q, 0),
                  _const_spec((1, nq)),
                  _const_spec((V7X_MXU_DIM, V7X_MXU_DIM)), *cast_in],
        out_specs=[rows(d), rows(nq), rows(nk), rows(nkv - nk), *cast_out],
        out_shape=[jax.ShapeDtypeStruct((m, d), _F32),
                   jax.ShapeDtypeStruct((m, nq), _BF16),
                   jax.ShapeDtypeStruct((m, nk), _BF16),
                   jax.ShapeDtypeStruct((m, nkv - nk), _BF16), *cast_shapes],
        compiler_params=pltpu.CompilerParams(
            dimension_semantics=("arbitrary",),
            vmem_limit_bytes=V7X_VMEM_LIMIT_BYTES),
        name="mlp_qkv",
    )(x, g_mlp, w_up, w_down, g_kv, w_kv, g_k, g_b, w_q, g_q, ones,
      *[w for w, _ in next_weights])
    return outs[:4], outs[4:]


def _attn_kernel(q_ref, k_ref, v_ref, lq1_ref, lk1_ref, lq2_ref, lk2_ref,
                 sub_ref, o_ref, s_scr, p_scr, vx_scr, *, lam_init):
    s_len = q_ref.shape[1]
    n_heads = q_ref.shape[2] // V_DIM
    tq, tk = Q_TILE, K_TILE
    lam = (jnp.exp(jnp.sum(lq1_ref[...] * lk1_ref[...], keepdims=True))
           - jnp.exp(jnp.sum(lq2_ref[...] * lk2_ref[...], keepdims=True))
           + lam_init)
    out_gain = sub_ref[...] * (1.0 - lam_init)

    for h in range(n_heads):
        vx_scr[h, :, 0:V_DIM] = v_ref[0, :, h * V_DIM:(h + 1) * V_DIM]
        vx_scr[h, :, V_DIM:2 * V_DIM] = jnp.ones((s_len, V_DIM), _BF16)

    lane = lax.broadcasted_iota(jnp.int32, (tq, V_DIM), 1)
    row_id = lax.broadcasted_iota(jnp.int32, (tq, tk), 0)
    row_id = jnp.concatenate([row_id, row_id], axis=0)
    col_id = lax.broadcasted_iota(jnp.int32, (2 * tq, tk), 1)

    def lane_tiles(a):
        return [a[:, t:t + V7X_LANES] for t in range(0, a.shape[1], V7X_LANES)]

    tiles = [(h, qi) for h in range(n_heads)
             for qi in reversed(range(s_len // tq))]
    row_max = {}

    def scores(pos):
        (h, qi), slot = tiles[pos], pos % 2
        q_lo, cols = qi * tq, slice(h * V_DIM, (h + 1) * V_DIM)
        q = q_ref[0, q_lo:q_lo + tq, cols]
        zero = jnp.zeros_like(q)
        q2 = jnp.concatenate([jnp.where(lane < HEAD_DIM, q, zero),
                              jnp.where(lane >= HEAD_DIM, q, zero)], axis=0)
        m_run = jnp.full((2 * tq, V7X_LANES), NEG_INF, _F32)
        for k_lo in range(0, q_lo + tq, tk):
            s = _dot_nt(q2, k_ref[0, k_lo:k_lo + tk, cols])
            if k_lo + tk > q_lo:
                visible = (k_lo + col_id) <= (q_lo + row_id)
                s = jnp.where(visible, s, NEG_INF)
            s_scr[slot, :, k_lo:k_lo + tk] = s
            for part in lane_tiles(s):
                m_run = jnp.maximum(m_run, part)
            yield
        row_max[pos] = jnp.max(m_run, axis=-1, keepdims=True)

    def probs(pos):
        (_, qi), slot = tiles[pos], pos % 2
        for k_lo in range(0, qi * tq + tq, tk):
            p = jnp.exp2(s_scr[slot, :, k_lo:k_lo + tk] - row_max[pos])
            p_scr[slot, :, k_lo:k_lo + tk] = p.astype(_BF16)
            yield

    def values(pos):
        (h, qi), slot = tiles[pos], pos % 2
        q_lo, n_keys = qi * tq, qi * tq + tq
        pv = _dot(p_scr[slot, :, 0:n_keys], vx_scr[h, 0:n_keys, :])
        maps = pv[:, 0:V_DIM] / pv[:, V_DIM:2 * V_DIM]
        o = maps[0:tq] - lam * maps[tq:2 * tq]
        o_ref[0, q_lo:q_lo + tq, h * V_DIM:(h + 1) * V_DIM] = (
            _rms(o, out_gain).astype(_BF16))

    for _ in scores(0):
        pass
    for pos in range(len(tiles) + 1):
        if pos >= 1:
            values(pos - 1)
        active = [stage(p) for stage, p in ((probs, pos), (scores, pos + 1))
                  if p < len(tiles)]
        while active:
            for gen in list(active):
                if next(gen, StopIteration) is StopIteration:
                    active.remove(gen)


def _attention(q, k, v, lq1, lk1, lq2, lk2, sub, lam_init):
    b, s, n = q.shape
    n_heads = n // V_DIM
    assert s % Q_TILE == 0 and s % K_TILE == 0 and n_heads % HEADS_PER_STEP == 0
    head = pl.BlockSpec((1, s, HEADS_PER_STEP * V_DIM), lambda i, h: (i, 0, h))
    vec = _const_spec((1, HEAD_DIM))
    return pl.pallas_call(
        functools.partial(_attn_kernel, lam_init=lam_init),
        grid=(b, n_heads // HEADS_PER_STEP),
        in_specs=[head, head, head, vec, vec, vec, vec, _const_spec((1, V_DIM))],
        out_specs=head,
        out_shape=jax.ShapeDtypeStruct((b, s, n), _BF16),
        scratch_shapes=[pltpu.VMEM((2, 2 * Q_TILE, s), _F32),
                        pltpu.VMEM((2, 2 * Q_TILE, s), _BF16),
                        pltpu.VMEM((HEADS_PER_STEP, s, 2 * V_DIM), _BF16)],
        compiler_params=pltpu.CompilerParams(
            dimension_semantics=("arbitrary", "arbitrary"),
            vmem_limit_bytes=V7X_VMEM_LIMIT_BYTES),
        name="attention",
    )(q, k, v, lq1, lk1, lq2, lk2, sub)


def _out_mlp_kernel(x_ref, o_ref, wo_ref, gm_ref, wup_ref, wdn_ref, y_ref):
    x2 = x_ref[...] + _dot(o_ref[...], wo_ref[...])
    y_ref[...] = _mlp(x2, gm_ref[...], wup_ref, wdn_ref)


def _out_mlp(x, o, w_o, g_mlp, w_up, w_down):
    m, d = x.shape
    n = o.shape[1]
    d_ff = w_up.shape[2]
    tm = BIG_ROW_TILE
    assert m % tm == 0 and d_ff % FF_CHUNK == 0
    return pl.pallas_call(
        _out_mlp_kernel,
        grid=(m // tm,),
        in_specs=[pl.BlockSpec((tm, d), lambda i: (i, 0)),
                  pl.BlockSpec((tm, n), lambda i: (i, 0)),
                  _layer_spec(w_o, 0), _const_spec((1, d)),
                  _layer_spec(w_up, 0), _layer_spec(w_down, 0)],
        out_specs=pl.BlockSpec((tm, d), lambda i: (i, 0)),
        out_shape=jax.ShapeDtypeStruct((m, d), _F32),
        compiler_params=pltpu.CompilerParams(
            dimension_semantics=("arbitrary",),
            vmem_limit_bytes=V7X_VMEM_LIMIT_BYTES),
        name="out_mlp",
    )(x, o, w_o, g_mlp, w_up, w_down)


def _lambda_init(layer_idx):
    return 0.8 - 0.6 * math.exp(-0.3 * layer_idx)


def kernel(x, a_norm, a_w_in, a_conv, a_w_out, kv_norm, w_kv, k_norm, b_norm,
           w_q, q_norm, lam_q1, lam_k1, lam_q2, lam_k2, sub_norm, w_o,
           mlp_norm, w_up, w_down):
    b, s, d = x.shape
    n_a, n_b = a_norm.shape[0], b_norm.shape[0]
    assert n_a == 1 and n_b == 1, "one mixer layer followed by one attention layer"
    n_qk = w_q.shape[2]
    assert w_kv.shape[1] == 2 * n_qk and n_qk % V7X_MXU_DIM == 0

    row = lambda g: g.reshape(1, -1)
    groups = n_qk // HEAD_DIM
    gid = jnp.arange(V7X_MXU_DIM) // HEAD_DIM
    ones = (gid[:, None] == gid[None, :]).astype(_BF16)

    x, (w_up_bf0, w_down_bf0, w_kv_bf, w_q_bf) = _mixer(
        x, row(a_norm[0]), a_w_in, a_conv[0], a_w_out, 0,
        [(w_up, 0), (w_down, 0), (w_kv[None], 0), (w_q, 0)])
    (x1, q, k, v), (w_o_bf, w_up_bf1, w_down_bf1) = _mlp_qkv(
        x.reshape(b * s, d), row(mlp_norm[0]), w_up_bf0, w_down_bf0,
        row(kv_norm), w_kv_bf, row(jnp.tile(k_norm, groups)),
        row(b_norm[0]), w_q_bf, row(jnp.tile(q_norm[0], groups)), ones,
        [(w_o, 0), (w_up, 1), (w_down, 1)])
    o = _attention(q.reshape(b, s, n_qk), k.reshape(b, s, n_qk),
                   v.reshape(b, s, -1), row(lam_q1[0]), row(lam_k1[0]),
                   row(lam_q2[0]), row(lam_k2[0]), row(sub_norm[0]),
                   _lambda_init(n_a))
    y = _out_mlp(x1, o.reshape(b * s, -1), w_o_bf, row(mlp_norm[1]),
                 w_up_bf1, w_down_bf1)
    return y.reshape(b, s, d)
```

```python
import functools
import math

import jax
import jax.numpy as jnp
from jax import lax
from jax.experimental import pallas as pl
from jax.experimental.pallas import tpu as pltpu

EPS = 1e-6
NEG_INF = -1e30
HEAD_DIM = 64
V_DIM = 2 * HEAD_DIM
CONV_WIDTH = 3

V7X_LANES = 128
V7X_SUBLANES = 8
V7X_MXU_DIM = 256
V7X_BF16_ROWS_PER_VREG = 16
V7X_VMEM_LIMIT_BYTES = 60 * 1024 * 1024

ROW_TILE = 512
BIG_ROW_TILE = 1024
FF_CHUNK = 1024
Q_TILE = 256
K_TILE = 256
HEADS_PER_STEP = 2
CAST_ROWS = 128
QKV_ROW_GROUPS = 2

_LOG2_E = math.log2(math.e)
_BF16 = jnp.bfloat16
_F32 = jnp.float32


def _dot(a, b):
    return jnp.dot(a, b, preferred_element_type=_F32)


def _dot_nt(a, b):
    return lax.dot_general(a, b, (((1,), (1,)), ((), ())),
                           preferred_element_type=_F32)


def _rms(x, g):
    ms = jnp.mean(x * x, axis=-1, keepdims=True)
    return x * lax.rsqrt(ms + EPS) * g


def _mlp(x, g, wup_ref, wdn_ref):
    h = _rms(x, g).astype(_BF16)
    acc = x
    d_ff = wup_ref.shape[1]
    for c in range(d_ff // FF_CHUNK):
        sl = slice(c * FF_CHUNK, (c + 1) * FF_CHUNK)
        a = jnp.maximum(_dot(h, wup_ref[:, sl]), 0.0)
        acc = acc + _dot((a * a).astype(_BF16), wdn_ref[sl, :])
    return acc


def _head_rms(y, ones_ref, gain):
    y2 = (y * y).astype(_BF16)
    n = y.shape[1]
    parts = [_dot(y2[:, t:t + V7X_MXU_DIM], ones_ref[...])
             for t in range(0, n, V7X_MXU_DIM)]
    ss = jnp.concatenate(parts, axis=1)
    return y * lax.rsqrt(ss * (1.0 / HEAD_DIM) + EPS) * gain


def _const_spec(shape):
    return pl.BlockSpec(shape, lambda *_: (0,) * len(shape),
                        pipeline_mode=pl.Buffered(1))


def _layer_spec(w, layer):
    _, r, c = w.shape
    return pl.BlockSpec((None, r, c), lambda *_: (layer, 0, 0),
                        pipeline_mode=pl.Buffered(1))


def _cast_jobs(weights, n_steps, step_of):
    in_specs, out_specs, shapes = [], [], []
    for w, layer in weights:
        _, r, c = w.shape
        assert r % (n_steps * V7X_BF16_ROWS_PER_VREG) == 0
        block = (None, r // n_steps, c)
        in_specs.append(pl.BlockSpec(
            block, lambda *ids, layer=layer: (layer, step_of(*ids), 0)))
        out_specs.append(pl.BlockSpec(
            block, lambda *ids: (0, step_of(*ids), 0)))
        shapes.append(jax.ShapeDtypeStruct((1, r, c), _BF16))
    return in_specs, out_specs, shapes


def _run_casts(src_refs, dst_refs):
    for src, dst in zip(src_refs, dst_refs):
        dst[...] = src[...].astype(_BF16)


def _mixer_kernel(*refs, n_cast):
    x_ref, g_ref, win_ref, conv_ref, wout_ref = refs[:5]
    cast_src = refs[5:5 + n_cast]
    o_ref = refs[5 + n_cast]
    cast_dst = refs[6 + n_cast:6 + 2 * n_cast]
    carry_ref, win_bf, wout_bf = refs[6 + 2 * n_cast:]
    d = x_ref.shape[2]
    ts = x_ref.shape[1]

    @pl.when((pl.program_id(0) == 0) & (pl.program_id(1) == 0))
    def _():
        def cast_rows(i, _):
            r = pl.ds(pl.multiple_of(i * CAST_ROWS, CAST_ROWS), CAST_ROWS)
            win_bf[r, :] = win_ref[r, :].astype(_BF16)
            wout_bf[r, :] = wout_ref[r, :].astype(_BF16)
            return 0
        lax.fori_loop(0, d // CAST_ROWS, cast_rows, 0)

    @pl.when(pl.program_id(1) == 0)
    def _():
        carry_ref[...] = jnp.zeros_like(carry_ref)

    _run_casts(cast_src, cast_dst)

    x = x_ref[0]
    h = _rms(x, g_ref[...]).astype(_BF16)
    b_gate = _dot(h, win_bf[:, 0:d])
    c_gate = _dot(h, win_bf[:, d:2 * d])
    v = _dot(h, win_bf[:, 2 * d:3 * d])
    u = c_gate * v

    rows = lax.broadcasted_iota(jnp.int32, u.shape, 0)
    tail = carry_ref[...]
    prev1 = tail[V7X_SUBLANES - 1:V7X_SUBLANES]
    prev2 = tail[V7X_SUBLANES - 2:V7X_SUBLANES - 1]
    u1 = jnp.where(rows == 0, prev1, pltpu.roll(u, 1, 0))
    u2 = jnp.where(rows == 0, prev2,
                   jnp.where(rows == 1, prev1, pltpu.roll(u, 2, 0)))
    w = conv_ref[...]
    y = w[0:1] * u2 + w[1:2] * u1 + w[2:3] * u
    carry_ref[...] = u[ts - V7X_SUBLANES:ts]

    o_ref[0] = x + _dot((b_gate * y).astype(_BF16), wout_bf[...])


def _mixer(x, g, w_in, conv_w, w_out, layer, next_weights):
    b, s, d = x.shape
    ts = BIG_ROW_TILE
    assert s % ts == 0 and d % CAST_ROWS == 0
    n_seq = s // ts
    tok = pl.BlockSpec((1, ts, d), lambda i, j: (i, j, 0))
    cast_in, cast_out, cast_shapes = _cast_jobs(next_weights, b * n_seq,
                                                lambda i, j: i * n_seq + j)
    outs = pl.pallas_call(
        functools.partial(_mixer_kernel, n_cast=len(cast_in)),
        grid=(b, n_seq),
        in_specs=[tok, _const_spec((1, d)), _layer_spec(w_in, layer),
                  _const_spec((CONV_WIDTH, d)), _layer_spec(w_out, layer),
                  *cast_in],
        out_specs=[tok, *cast_out],
        out_shape=[jax.ShapeDtypeStruct(x.shape, _F32), *cast_shapes],
        scratch_shapes=[pltpu.VMEM((V7X_SUBLANES, d), _F32),
                        pltpu.VMEM(w_in.shape[1:], _BF16),
                        pltpu.VMEM(w_out.shape[1:], _BF16)],
        compiler_params=pltpu.CompilerParams(
            dimension_semantics=("arbitrary", "arbitrary"),
            vmem_limit_bytes=V7X_VMEM_LIMIT_BYTES),
        name="mixer",
    )(x, g, w_in, conv_w, w_out, *[w for w, _ in next_weights])
    return outs[0], outs[1:]


def _mlp_qkv_kernel(*refs, n_cast):
    (x_ref, gm_ref, wup_ref, wdn_ref, gkv_ref, wkv_ref, gk_ref, gb_ref, wq_ref,
     gq_ref, ones_ref) = refs[:11]
    cast_src = refs[11:11 + n_cast]
    x1_ref, q_ref, k_ref, v_ref = refs[11 + n_cast:15 + n_cast]
    cast_dst = refs[15 + n_cast:]

    _run_casts(cast_src, cast_dst)

    nk = k_ref.shape[1]
    sub = x_ref.shape[0] // QKV_ROW_GROUPS
    groups = [slice(r * sub, (r + 1) * sub) for r in range(QKV_ROW_GROUPS)]
    x1s = []
    for rows in groups:
        x1 = _mlp(x_ref[rows, :], gm_ref[...], wup_ref, wdn_ref)
        x1_ref[rows, :] = x1
        x1s.append(x1)
    for rows, x1 in zip(groups, x1s):
        hq = _rms(x1, gb_ref[...]).astype(_BF16)
        q = _head_rms(_dot(hq, wq_ref[...]), ones_ref, gq_ref[...])
        q_ref[rows, :] = (q * (HEAD_DIM ** -0.5 * _LOG2_E)).astype(_BF16)

        hkv = _rms(x1, gkv_ref[...]).astype(_BF16)
        k = _dot(hkv, wkv_ref[:, 0:nk])
        k_ref[rows, :] = _head_rms(k, ones_ref, gk_ref[...]).astype(_BF16)
        v_ref[rows, :] = _dot(hkv, wkv_ref[:, nk:]).astype(_BF16)


def _mlp_qkv(x, g_mlp, w_up, w_down, g_kv, w_kv, g_k, g_b, w_q, g_q, ones,
             next_weights):
    m, d = x.shape
    d_ff = w_up.shape[2]
    nq = w_q.shape[2]
    nkv = w_kv.shape[2]
    nk = nq
    tm = ROW_TILE
    assert m % tm == 0 and d_ff % FF_CHUNK == 0

    def rows(n):
        return pl.BlockSpec((tm, n), lambda i: (i, 0))

    cast_in, cast_out, cast_shapes = _cast_jobs(next_weights, m // tm,
                                                lambda i: i)
    outs = pl.pallas_call(
        functools.partial(_mlp_qkv_kernel, n_cast=len(cast_in)),
        grid=(m // tm,),
        in_specs=[rows(d), _const_spec((1, d)), _layer_spec(w_up, 0),
                  _layer_spec(w_down, 0), _const_spec((1, d)),
                  _layer_spec(w_kv, 0), _const_spec((1, nk)),
                  _const_spec((1, d)), _layer_spec(w_q, 0),
                  _const_spec((1, nq)),
                  _const_spec((V7X_MXU_DIM, V7X_MXU_DIM)), *cast_in],
        out_specs=[rows(d), rows(nq), rows(nk), rows(nkv - nk), *cast_out],
        out_shape=[jax.ShapeDtypeStruct((m, d), _F32),
                   jax.ShapeDtypeStruct((m, nq), _BF16),
                   jax.ShapeDtypeStruct((m, nk), _BF16),
                   jax.ShapeDtypeStruct((m, nkv - nk), _BF16), *cast_shapes],
        compiler_params=pltpu.CompilerParams(
            dimension_semantics=("arbitrary",),
            vmem_limit_bytes=V7X_VMEM_LIMIT_BYTES),
        name="mlp_qkv",
    )(x, g_mlp, w_up, w_down, g_kv, w_kv, g_k, g_b, w_q, g_q, ones,
      *[w for w, _ in next_weights])
    return outs[:4], outs[4:]


def _attn_kernel(q_ref, k_ref, v_ref, lq1_ref, lk1_ref, lq2_ref, lk2_ref,
                 sub_ref, o_ref, s_scr, p_scr, vx_scr, *, lam_init):
    s_len = q_ref.shape[1]
    n_heads = q_ref.shape[2] // V_DIM
    tq, tk = Q_TILE, K_TILE
    lam = (jnp.exp(jnp.sum(lq1_ref[...] * lk1_ref[...], keepdims=True))
           - jnp.exp(jnp.sum(lq2_ref[...] * lk2_ref[...], keepdims=True))
           + lam_init)
    out_gain = sub_ref[...] * (1.0 - lam_init)

    for h in range(n_heads):
        vx_scr[h, :, 0:V_DIM] = v_ref[0, :, h * V_DIM:(h + 1) * V_DIM]
        vx_scr[h, :, V_DIM:2 * V_DIM] = jnp.ones((s_len, V_DIM), _BF16)

    lane = lax.broadcasted_iota(jnp.int32, (tq, V_DIM), 1)
    row_id = lax.broadcasted_iota(jnp.int32, (tq, tk), 0)
    row_id = jnp.concatenate([row_id, row_id], axis=0)
    col_id = lax.broadcasted_iota(jnp.int32, (2 * tq, tk), 1)

    def lane_tiles(a):
        return [a[:, t:t + V7X_LANES] for t in range(0, a.shape[1], V7X_LANES)]

    tiles = [(h, qi) for h in range(n_heads)
             for qi in reversed(range(s_len // tq))]
    row_max = {}

    def scores(pos):
        (h, qi), slot = tiles[pos], pos % 2
        q_lo, cols = qi * tq, slice(h * V_DIM, (h + 1) * V_DIM)
        q = q_ref[0, q_lo:q_lo + tq, cols]
        zero = jnp.zeros_like(q)
        q2 = jnp.concatenate([jnp.where(lane < HEAD_DIM, q, zero),
                              jnp.where(lane >= HEAD_DIM, q, zero)], axis=0)
        m_run = jnp.full((2 * tq, V7X_LANES), NEG_INF, _F32)
        for k_lo in range(0, q_lo + tq, tk):
            s = _dot_nt(q2, k_ref[0, k_lo:k_lo + tk, cols])
            if k_lo + tk > q_lo:
                visible = (k_lo + col_id) <= (q_lo + row_id)
                s = jnp.where(visible, s, NEG_INF)
            s_scr[slot, :, k_lo:k_lo + tk] = s
            for part in lane_tiles(s):
                m_run = jnp.maximum(m_run, part)
            yield
        row_max[pos] = jnp.max(m_run, axis=-1, keepdims=True)

    def probs(pos):
        (_, qi), slot = tiles[pos], pos % 2
        for k_lo in range(0, qi * tq + tq, tk):
            p = jnp.exp2(s_scr[slot, :, k_lo:k_lo + tk] - row_max[pos])
            p_scr[slot, :, k_lo:k_lo + tk] = p.astype(_BF16)
            yield

    def values(pos):
        (h, qi), slot = tiles[pos], pos % 2
        q_lo, n_keys = qi * tq, qi * tq + tq
        pv = _dot(p_scr[slot, :, 0:n_keys], vx_scr[h, 0:n_keys, :])
        maps = pv[:, 0:V_DIM] / pv[:, V_DIM:2 * V_DIM]
        o = maps[0:tq] - lam * maps[tq:2 * tq]
        o_ref[0, q_lo:q_lo + tq, h * V_DIM:(h + 1) * V_DIM] = (
            _rms(o, out_gain).astype(_BF16))

    for _ in scores(0):
        pass
    for pos in range(len(tiles) + 1):
        if pos >= 1:
            values(pos - 1)
        active = [stage(p) for stage, p in ((probs, pos), (scores, pos + 1))
                  if p < len(tiles)]
        while active:
            for gen in list(active):
                if next(gen, StopIteration) is StopIteration:
                    active.remove(gen)


def _attention(q, k, v, lq1, lk1, lq2, lk2, sub, lam_init):
    b, s, n = q.shape
    n_heads = n // V_DIM
    assert s % Q_TILE == 0 and s % K_TILE == 0 and n_heads % HEADS_PER_STEP == 0
    head = pl.BlockSpec((1, s, HEADS_PER_STEP * V_DIM), lambda i, h: (i, 0, h))
    vec = _const_spec((1, HEAD_DIM))
    return pl.pallas_call(
        functools.partial(_attn_kernel, lam_init=lam_init),
        grid=(b, n_heads // HEADS_PER_STEP),
        in_specs=[head, head, head, vec, vec, vec, vec, _const_spec((1, V_DIM))],
        out_specs=head,
        out_shape=jax.ShapeDtypeStruct((b, s, n), _BF16),
        scratch_shapes=[pltpu.VMEM((2, 2 * Q_TILE, s), _F32),
                        pltpu.VMEM((2, 2 * Q_TILE, s), _BF16),
                        pltpu.VMEM((HEADS_PER_STEP, s, 2 * V_DIM), _BF16)],
        compiler_params=pltpu.CompilerParams(
            dimension_semantics=("arbitrary", "arbitrary"),
            vmem_limit_bytes=V7X_VMEM_LIMIT_BYTES),
        name="attention",
    )(q, k, v, lq1, lk1, lq2, lk2, sub)


def _out_mlp_kernel(x_ref, o_ref, wo_ref, gm_ref, wup_ref, wdn_ref, y_ref):
    x2 = x_ref[...] + _dot(o_ref[...], wo_ref[...])
    y_ref[...] = _mlp(x2, gm_ref[...], wup_ref, wdn_ref)


def _out_mlp(x, o, w_o, g_mlp, w_up, w_down):
    m, d = x.shape
    n = o.shape[1]
    d_ff = w_up.shape[2]
    tm = BIG_ROW_TILE
    assert m % tm == 0 and d_ff % FF_CHUNK == 0
    return pl.pallas_call(
        _out_mlp_kernel,
        grid=(m // tm,),
        in_specs=[pl.BlockSpec((tm, d), lambda i: (i, 0)),
                  pl.BlockSpec((tm, n), lambda i: (i, 0)),
                  _layer_spec(w_o, 0), _const_spec((1, d)),
                  _layer_spec(w_up, 0), _layer_spec(w_down, 0)],
        out_specs=pl.BlockSpec((tm, d), lambda i: (i, 0)),
        out_shape=jax.ShapeDtypeStruct((m, d), _F32),
        compiler_params=pltpu.CompilerParams(
            dimension_semantics=("arbitrary",),
            vmem_limit_bytes=V7X_VMEM_LIMIT_BYTES),
        name="out_mlp",
    )(x, o, w_o, g_mlp, w_up, w_down)


def _lambda_init(layer_idx):
    return 0.8 - 0.6 * math.exp(-0.3 * layer_idx)


def kernel(x, a_norm, a_w_in, a_conv, a_w_out, kv_norm, w_kv, k_norm, b_norm,
           w_q, q_norm, lam_q1, lam_k1, lam_q2, lam_k2, sub_norm, w_o,
           mlp_norm, w_up, w_down):
    b, s, d = x.shape
    n_a, n_b = a_norm.shape[0], b_norm.shape[0]
    assert n_a == 1 and n_b == 1, "one mixer layer followed by one attention layer"
    n_qk = w_q.shape[2]
    assert w_kv.shape[1] == 2 * n_qk and n_qk % V7X_MXU_DIM == 0

    row = lambda g: g.reshape(1, -1)
    groups = n_qk // HEAD_DIM
    gid = jnp.arange(V7X_MXU_DIM) // HEAD_DIM
    ones = (gid[:, None] == gid[None, :]).astype(_BF16)

    x, (w_up_bf0, w_down_bf0, w_kv_bf, w_q_bf) = _mixer(
        x, row(a_norm[0]), a_w_in, a_conv[0], a_w_out, 0,
        [(w_up, 0), (w_down, 0), (w_kv[None], 0), (w_q, 0)])
    (x1, q, k, v), (w_o_bf, w_up_bf1, w_down_bf1) = _mlp_qkv(
        x.reshape(b * s, d), row(mlp_norm[0]), w_up_bf0, w_down_bf0,
        row(kv_norm), w_kv_bf, row(jnp.tile(k_norm, groups)),
        row(b_norm[0]), w_q_bf, row(jnp.tile(q_norm[0], groups)), ones,
        [(w_o, 0), (w_up, 1), (w_down, 1)])
    o = _attention(q.reshape(b, s, n_qk), k.reshape(b, s, n_qk),
                   v.reshape(b, s, -1), row(lam_q1[0]), row(lam_k1[0]),
                   row(lam_q2[0]), row(lam_k2[0]), row(sub_norm[0]),
                   _lambda_init(n_a))
    y = _out_mlp(x1, o.reshape(b * s, -1), w_o_bf, row(mlp_norm[1]),
                 w_up_bf1, w_down_bf1)
    return y.reshape(b, s, d)
```

```python
import functools
import math

import jax
import jax.numpy as jnp
from jax import lax
from jax.experimental import pallas as pl
from jax.experimental.pallas import tpu as pltpu

EPS = 1e-6
NEG_INF = -1e30
HEAD_DIM = 64
V_DIM = 2 * HEAD_DIM
CONV_WIDTH = 3

V7X_LANES = 128
V7X_SUBLANES = 8
V7X_MXU_DIM = 256
V7X_BF16_ROWS_PER_VREG = 16
V7X_VMEM_LIMIT_BYTES = 60 * 1024 * 1024

ROW_TILE = 512
BIG_ROW_TILE = 1024
FF_CHUNK = 2048
Q_TILE = 256
K_TILE = 256
HEADS_PER_STEP = 2
CAST_ROWS = 128

_LOG2_E = math.log2(math.e)
_BF16 = jnp.bfloat16
_F32 = jnp.float32


def _dot(a, b):
    return jnp.dot(a, b, preferred_element_type=_F32)


def _dot_nt(a, b):
    return lax.dot_general(a, b, (((1,), (1,)), ((), ())),
                           preferred_element_type=_F32)


def _rms(x, g):
    ms = jnp.mean(x * x, axis=-1, keepdims=True)
    return x * lax.rsqrt(ms + EPS) * g


def _mlp(x, g, wup_ref, wdn_ref):
    h = _rms(x, g).astype(_BF16)
    acc = x
    d_ff = wup_ref.shape[1]
    for c in range(d_ff // FF_CHUNK):
        sl = slice(c * FF_CHUNK, (c + 1) * FF_CHUNK)
        a = jnp.maximum(_dot(h, wup_ref[:, sl]), 0.0)
        acc = acc + _dot((a * a).astype(_BF16), wdn_ref[sl, :])
    return acc


def _head_rms(y, ones_ref, gain):
    y2 = (y * y).astype(_BF16)
    n = y.shape[1]
    parts = [_dot(y2[:, t:t + V7X_MXU_DIM], ones_ref[...])
             for t in range(0, n, V7X_MXU_DIM)]
    ss = jnp.concatenate(parts, axis=1)
    return y * lax.rsqrt(ss * (1.0 / HEAD_DIM) + EPS) * gain


def _const_spec(shape):
    return pl.BlockSpec(shape, lambda *_: (0,) * len(shape),
                        pipeline_mode=pl.Buffered(1))


def _layer_spec(w, layer):
    _, r, c = w.shape
    return pl.BlockSpec((None, r, c), lambda *_: (layer, 0, 0),
                        pipeline_mode=pl.Buffered(1))


def _cast_jobs(weights, n_steps, step_of):
    in_specs, out_specs, shapes = [], [], []
    for w, layer in weights:
        _, r, c = w.shape
        assert r % (n_steps * V7X_BF16_ROWS_PER_VREG) == 0
        block = (None, r // n_steps, c)
        in_specs.append(pl.BlockSpec(
            block, lambda *ids, layer=layer: (layer, step_of(*ids), 0)))
        out_specs.append(pl.BlockSpec(
            block, lambda *ids: (0, step_of(*ids), 0)))
        shapes.append(jax.ShapeDtypeStruct((1, r, c), _BF16))
    return in_specs, out_specs, shapes


def _run_casts(src_refs, dst_refs):
    for src, dst in zip(src_refs, dst_refs):
        dst[...] = src[...].astype(_BF16)


def _mixer_kernel(*refs, n_cast):
    x_ref, g_ref, win_ref, conv_ref, wout_ref = refs[:5]
    cast_src = refs[5:5 + n_cast]
    o_ref = refs[5 + n_cast]
    cast_dst = refs[6 + n_cast:6 + 2 * n_cast]
    carry_ref, win_bf, wout_bf = refs[6 + 2 * n_cast:]
    d = x_ref.shape[2]
    ts = x_ref.shape[1]

    @pl.when((pl.program_id(0) == 0) & (pl.program_id(1) == 0))
    def _():
        def cast_rows(i, _):
            r = pl.ds(pl.multiple_of(i * CAST_ROWS, CAST_ROWS), CAST_ROWS)
            win_bf[r, :] = win_ref[r, :].astype(_BF16)
            wout_bf[r, :] = wout_ref[r, :].astype(_BF16)
            return 0
        lax.fori_loop(0, d // CAST_ROWS, cast_rows, 0)

    @pl.when(pl.program_id(1) == 0)
    def _():
        carry_ref[...] = jnp.zeros_like(carry_ref)

    _run_casts(cast_src, cast_dst)

    x = x_ref[0]
    h = _rms(x, g_ref[...]).astype(_BF16)
    b_gate = _dot(h, win_bf[:, 0:d])
    c_gate = _dot(h, win_bf[:, d:2 * d])
    v = _dot(h, win_bf[:, 2 * d:3 * d])
    u = c_gate * v

    rows = lax.broadcasted_iota(jnp.int32, u.shape, 0)
    tail = carry_ref[...]
    prev1 = tail[V7X_SUBLANES - 1:V7X_SUBLANES]
    prev2 = tail[V7X_SUBLANES - 2:V7X_SUBLANES - 1]
    u1 = jnp.where(rows == 0, prev1, pltpu.roll(u, 1, 0))
    u2 = jnp.where(rows == 0, prev2,
                   jnp.where(rows == 1, prev1, pltpu.roll(u, 2, 0)))
    w = conv_ref[...]
    y = w[0:1] * u2 + w[1:2] * u1 + w[2:3] * u
    carry_ref[...] = u[ts - V7X_SUBLANES:ts]

    o_ref[0] = x + _dot((b_gate * y).astype(_BF16), wout_bf[...])


def _mixer(x, g, w_in, conv_w, w_out, layer, next_weights):
    b, s, d = x.shape
    ts = BIG_ROW_TILE
    assert s % ts == 0 and d % CAST_ROWS == 0
    n_seq = s // ts
    tok = pl.BlockSpec((1, ts, d), lambda i, j: (i, j, 0))
    cast_in, cast_out, cast_shapes = _cast_jobs(next_weights, b * n_seq,
                                                lambda i, j: i * n_seq + j)
    outs = pl.pallas_call(
        functools.partial(_mixer_kernel, n_cast=len(cast_in)),
        grid=(b, n_seq),
        in_specs=[tok, _const_spec((1, d)), _layer_spec(w_in, layer),
                  _const_spec((CONV_WIDTH, d)), _layer_spec(w_out, layer),
                  *cast_in],
        out_specs=[tok, *cast_out],
        out_shape=[jax.ShapeDtypeStruct(x.shape, _F32), *cast_shapes],
        scratch_shapes=[pltpu.VMEM((V7X_SUBLANES, d), _F32),
                        pltpu.VMEM(w_in.shape[1:], _BF16),
                        pltpu.VMEM(w_out.shape[1:], _BF16)],
        compiler_params=pltpu.CompilerParams(
            dimension_semantics=("arbitrary", "arbitrary"),
            vmem_limit_bytes=V7X_VMEM_LIMIT_BYTES),
        name="mixer",
    )(x, g, w_in, conv_w, w_out, *[w for w, _ in next_weights])
    return outs[0], outs[1:]


def _mlp_qkv_kernel(*refs, n_cast):
    (x_ref, gm_ref, wup_ref, wdn_ref, gkv_ref, wkv_ref, gk_ref, gb_ref, wq_ref,
     gq_ref, ones_ref) = refs[:11]
    cast_src = refs[11:11 + n_cast]
    x1_ref, q_ref, k_ref, v_ref = refs[11 + n_cast:15 + n_cast]
    cast_dst = refs[15 + n_cast:]

    _run_casts(cast_src, cast_dst)

    x1 = _mlp(x_ref[...], gm_ref[...], wup_ref, wdn_ref)
    x1_ref[...] = x1

    nk = k_ref.shape[1]
    hkv = _rms(x1, gkv_ref[...]).astype(_BF16)
    k = _dot(hkv, wkv_ref[:, 0:nk])
    k_ref[...] = _head_rms(k, ones_ref, gk_ref[...]).astype(_BF16)
    v_ref[...] = _dot(hkv, wkv_ref[:, nk:]).astype(_BF16)

    hq = _rms(x1, gb_ref[...]).astype(_BF16)
    q = _head_rms(_dot(hq, wq_ref[...]), ones_ref, gq_ref[...])
    q_ref[...] = (q * (HEAD_DIM ** -0.5 * _LOG2_E)).astype(_BF16)


def _mlp_qkv(x, g_mlp, w_up, w_down, g_kv, w_kv, g_k, g_b, w_q, g_q, ones,
             next_weights):
    m, d = x.shape
    d_ff = w_up.shape[2]
    nq = w_q.shape[2]
    nkv = w_kv.shape[2]
    nk = nq
    tm = ROW_TILE
    assert m % tm == 0 and d_ff % FF_CHUNK == 0

    def rows(n):
        return pl.BlockSpec((tm, n), lambda i: (i, 0))

    cast_in, cast_out, cast_shapes = _cast_jobs(next_weights, m // tm,
                                                lambda i: i)
    outs = pl.pallas_call(
        functools.partial(_mlp_qkv_kernel, n_cast=len(cast_in)),
        grid=(m // tm,),
        in_specs=[rows(d), _const_spec((1, d)), _layer_spec(w_up, 0),
                  _layer_spec(w_down, 0), _const_spec((1, d)),
                  _layer_spec(w_kv, 0), _const_spec((1, nk)),
                  _const_spec((1, d)), _layer_spec(w_q, 0),
                  _const_spec((1, nq)),
                  _const_spec((V7X_MXU_DIM, V7X_MXU_DIM)), *cast_in],
        out_specs=[rows(d), rows(nq), rows(nk), rows(nkv - nk), *cast_out],
        out_shape=[jax.ShapeDtypeStruct((m, d), _F32),
                   jax.ShapeDtypeStruct((m, nq), _BF16),
                   jax.ShapeDtypeStruct((m, nk), _BF16),
                   jax.ShapeDtypeStruct((m, nkv - nk), _BF16), *cast_shapes],
        compiler_params=pltpu.CompilerParams(
            dimension_semantics=("arbitrary",),
            vmem_limit_bytes=V7X_VMEM_LIMIT_BYTES),
        name="mlp_qkv",
    )(x, g_mlp, w_up, w_down, g_kv, w_kv, g_k, g_b, w_q, g_q, ones,
      *[w for w, _ in next_weights])
    return outs[:4], outs[4:]


def _attn_kernel(q_ref, k_ref, v_ref, lq1_ref, lk1_ref, lq2_ref, lk2_ref,
                 sub_ref, o_ref, s_scr, p_scr, vx_scr, *, lam_init):
    s_len = q_ref.shape[1]
    n_heads = q_ref.shape[2] // V_DIM
    tq, tk = Q_TILE, K_TILE
    lam = (jnp.exp(jnp.sum(lq1_ref[...] * lk1_ref[...], keepdims=True))
           - jnp.exp(jnp.sum(lq2_ref[...] * lk2_ref[...], keepdims=True))
           + lam_init)
    out_gain = sub_ref[...] * (1.0 - lam_init)

    for h in range(n_heads):
        vx_scr[h, :, 0:V_DIM] = v_ref[0, :, h * V_DIM:(h + 1) * V_DIM]
        vx_scr[h, :, V_DIM:2 * V_DIM] = jnp.ones((s_len, V_DIM), _BF16)

    lane = lax.broadcasted_iota(jnp.int32, (tq, V_DIM), 1)
    row_id = lax.broadcasted_iota(jnp.int32, (tq, tk), 0)
    row_id = jnp.concatenate([row_id, row_id], axis=0)
    col_id = lax.broadcasted_iota(jnp.int32, (2 * tq, tk), 1)

    def lane_tiles(a):
        return [a[:, t:t + V7X_LANES] for t in range(0, a.shape[1], V7X_LANES)]

    tiles = [(h, qi) for h in range(n_heads)
             for qi in reversed(range(s_len // tq))]
    row_max = {}

    def scores(pos):
        (h, qi), slot = tiles[pos], pos % 2
        q_lo, cols = qi * tq, slice(h * V_DIM, (h + 1) * V_DIM)
        q = q_ref[0, q_lo:q_lo + tq, cols]
        zero = jnp.zeros_like(q)
        q2 = jnp.concatenate([jnp.where(lane < HEAD_DIM, q, zero),
                              jnp.where(lane >= HEAD_DIM, q, zero)], axis=0)
        m_run = jnp.full((2 * tq, V7X_LANES), NEG_INF, _F32)
        for k_lo in range(0, q_lo + tq, tk):
            s = _dot_nt(q2, k_ref[0, k_lo:k_lo + tk, cols])
            if k_lo + tk > q_lo:
                visible = (k_lo + col_id) <= (q_lo + row_id)
                s = jnp.where(visible, s, NEG_INF)
            s_scr[slot, :, k_lo:k_lo + tk] = s
            for part in lane_tiles(s):
                m_run = jnp.maximum(m_run, part)
            yield
        row_max[pos] = jnp.max(m_run, axis=-1, keepdims=True)

    def probs(pos):
        (_, qi), slot = tiles[pos], pos % 2
        for k_lo in range(0, qi * tq + tq, tk):
            p = jnp.exp2(s_scr[slot, :, k_lo:k_lo + tk] - row_max[pos])
            p_scr[slot, :, k_lo:k_lo + tk] = p.astype(_BF16)
            yield

    def values(pos):
        (h, qi), slot = tiles[pos], pos % 2
        q_lo, n_keys = qi * tq, qi * tq + tq
        pv = _dot(p_scr[slot, :, 0:n_keys], vx_scr[h, 0:n_keys, :])
        maps = pv[:, 0:V_DIM] / pv[:, V_DIM:2 * V_DIM]
        o = maps[0:tq] - lam * maps[tq:2 * tq]
        o_ref[0, q_lo:q_lo + tq, h * V_DIM:(h + 1) * V_DIM] = (
            _rms(o, out_gain).astype(_BF16))

    for _ in scores(0):
        pass
    for pos in range(len(tiles) + 1):
        if pos >= 1:
            values(pos - 1)
        active = [stage(p) for stage, p in ((probs, pos), (scores, pos + 1))
                  if p < len(tiles)]
        while active:
            for gen in list(active):
                if next(gen, StopIteration) is StopIteration:
                    active.remove(gen)


def _attention(q, k, v, lq1, lk1, lq2, lk2, sub, lam_init):
    b, s, n = q.shape
    n_heads = n // V_DIM
    assert s % Q_TILE == 0 and s % K_TILE == 0 and n_heads % HEADS_PER_STEP == 0
    head = pl.BlockSpec((1, s, HEADS_PER_STEP * V_DIM), lambda i, h: (i, 0, h))
    vec = _const_spec((1, HEAD_DIM))
    return pl.pallas_call(
        functools.partial(_attn_kernel, lam_init=lam_init),
        grid=(b, n_heads // HEADS_PER_STEP),
        in_specs=[head, head, head, vec, vec, vec, vec, _const_spec((1, V_DIM))],
        out_specs=head,
        out_shape=jax.ShapeDtypeStruct((b, s, n), _BF16),
        scratch_shapes=[pltpu.VMEM((2, 2 * Q_TILE, s), _F32),
                        pltpu.VMEM((2, 2 * Q_TILE, s), _BF16),
                        pltpu.VMEM((HEADS_PER_STEP, s, 2 * V_DIM), _BF16)],
        compiler_params=pltpu.CompilerParams(
            dimension_semantics=("arbitrary", "arbitrary"),
            vmem_limit_bytes=V7X_VMEM_LIMIT_BYTES),
        name="attention",
    )(q, k, v, lq1, lk1, lq2, lk2, sub)


def _out_mlp_kernel(x_ref, o_ref, wo_ref, gm_ref, wup_ref, wdn_ref, y_ref):
    x2 = x_ref[...] + _dot(o_ref[...], wo_ref[...])
    y_ref[...] = _mlp(x2, gm_ref[...], wup_ref, wdn_ref)


def _out_mlp(x, o, w_o, g_mlp, w_up, w_down):
    m, d = x.shape
    n = o.shape[1]
    d_ff = w_up.shape[2]
    tm = BIG_ROW_TILE
    assert m % tm == 0 and d_ff % FF_CHUNK == 0
    return pl.pallas_call(
        _out_mlp_kernel,
        grid=(m // tm,),
        in_specs=[pl.BlockSpec((tm, d), lambda i: (i, 0)),
                  pl.BlockSpec((tm, n), lambda i: (i, 0)),
                  _layer_spec(w_o, 0), _const_spec((1, d)),
                  _layer_spec(w_up, 0), _layer_spec(w_down, 0)],
        out_specs=pl.BlockSpec((tm, d), lambda i: (i, 0)),
        out_shape=jax.ShapeDtypeStruct((m, d), _F32),
        compiler_params=pltpu.CompilerParams(
            dimension_semantics=("arbitrary",),
            vmem_limit_bytes=V7X_VMEM_LIMIT_BYTES),
        name="out_mlp",
    )(x, o, w_o, g_mlp, w_up, w_down)


def _lambda_init(layer_idx):
    return 0.8 - 0.6 * math.exp(-0.3 * layer_idx)


def kernel(x, a_norm, a_w_in, a_conv, a_w_out, kv_norm, w_kv, k_norm, b_norm,
           w_q, q_norm, lam_q1, lam_k1, lam_q2, lam_k2, sub_norm, w_o,
           mlp_norm, w_up, w_down):
    b, s, d = x.shape
    n_a, n_b = a_norm.shape[0], b_norm.shape[0]
    assert n_a == 1 and n_b == 1, "one mixer layer followed by one attention layer"
    n_qk = w_q.shape[2]
    assert w_kv.shape[1] == 2 * n_qk and n_qk % V7X_MXU_DIM == 0

    row = lambda g: g.reshape(1, -1)
    groups = n_qk // HEAD_DIM
    gid = jnp.arange(V7X_MXU_DIM) // HEAD_DIM
    ones = (gid[:, None] == gid[None, :]).astype(_BF16)

    x, (w_up_bf0, w_down_bf0, w_kv_bf, w_q_bf) = _mixer(
        x, row(a_norm[0]), a_w_in, a_conv[0], a_w_out, 0,
        [(w_up, 0), (w_down, 0), (w_kv[None], 0), (w_q, 0)])
    (x1, q, k, v), (w_o_bf, w_up_bf1, w_down_bf1) = _mlp_qkv(
        x.reshape(b * s, d), row(mlp_norm[0]), w_up_bf0, w_down_bf0,
        row(kv_norm), w_kv_bf, row(jnp.tile(k_norm, groups)),
        row(b_norm[0]), w_q_bf, row(jnp.tile(q_norm[0], groups)), ones,
        [(w_o, 0), (w_up, 1), (w_down, 1)])
    o = _attention(q.reshape(b, s, n_qk), k.reshape(b, s, n_qk),
                   v.reshape(b, s, -1), row(lam_q1[0]), row(lam_k1[0]),
                   row(lam_q2[0]), row(lam_k2[0]), row(sub_norm[0]),
                   _lambda_init(n_a))
    y = _out_mlp(x1, o.reshape(b * s, -1), w_o_bf, row(mlp_norm[1]),
                 w_up_bf1, w_down_bf1)
    return y.reshape(b, s, d)
```

```python
import functools
import math

import jax
import jax.numpy as jnp
from jax import lax
from jax.experimental import pallas as pl
from jax.experimental.pallas import tpu as pltpu

EPS = 1e-6
NEG_INF = -1e30
HEAD_DIM = 64
V_DIM = 2 * HEAD_DIM
CONV_WIDTH = 3

V7X_LANES = 128
V7X_SUBLANES = 8
V7X_MXU_DIM = 256
V7X_BF16_ROWS_PER_VREG = 16
V7X_VMEM_LIMIT_BYTES = 60 * 1024 * 1024
ATTN_VMEM_LIMIT_BYTES = 24 * 1024 * 1024
OUT_MLP_VMEM_LIMIT_BYTES = 34 * 1024 * 1024

ROW_TILE = 512
BIG_ROW_TILE = 1024
FF_CHUNK = 1024
Q_TILE = 256
K_TILE = 256
HEADS_PER_STEP = 2
CAST_ROWS = 128

_LOG2_E = math.log2(math.e)
_BF16 = jnp.bfloat16
_F32 = jnp.float32


def _dot(a, b):
    return jnp.dot(a, b, preferred_element_type=_F32)


def _dot_nt(a, b):
    return lax.dot_general(a, b, (((1,), (1,)), ((), ())),
                           preferred_element_type=_F32)


def _rms(x, g):
    ms = jnp.mean(x * x, axis=-1, keepdims=True)
    return x * lax.rsqrt(ms + EPS) * g


def _mlp(x, g, wup_ref, wdn_ref):
    h = _rms(x, g).astype(_BF16)
    acc = x
    d_ff = wup_ref.shape[1]
    for c in range(d_ff // FF_CHUNK):
        sl = slice(c * FF_CHUNK, (c + 1) * FF_CHUNK)
        a = jnp.maximum(_dot(h, wup_ref[:, sl]), 0.0)
        acc = acc + _dot((a * a).astype(_BF16), wdn_ref[sl, :])
    return acc


def _head_rms(y, ones_ref, gain):
    y2 = (y * y).astype(_BF16)
    n = y.shape[1]
    parts = [_dot(y2[:, t:t + V7X_MXU_DIM], ones_ref[...])
             for t in range(0, n, V7X_MXU_DIM)]
    ss = jnp.concatenate(parts, axis=1)
    return y * lax.rsqrt(ss * (1.0 / HEAD_DIM) + EPS) * gain


def _const_spec(shape):
    return pl.BlockSpec(shape, lambda *_: (0,) * len(shape),
                        pipeline_mode=pl.Buffered(1))


def _layer_spec(w, layer):
    _, r, c = w.shape
    return pl.BlockSpec((None, r, c), lambda *_: (layer, 0, 0),
                        pipeline_mode=pl.Buffered(1))


def _cast_jobs(weights, n_steps, step_of):
    in_specs, out_specs, shapes = [], [], []
    for w, layer in weights:
        _, r, c = w.shape
        assert r % (n_steps * V7X_BF16_ROWS_PER_VREG) == 0
        block = (None, r // n_steps, c)
        in_specs.append(pl.BlockSpec(
            block, lambda *ids, layer=layer: (layer, step_of(*ids), 0)))
        out_specs.append(pl.BlockSpec(
            block, lambda *ids: (0, step_of(*ids), 0)))
        shapes.append(jax.ShapeDtypeStruct((1, r, c), _BF16))
    return in_specs, out_specs, shapes


def _run_casts(src_refs, dst_refs):
    for src, dst in zip(src_refs, dst_refs):
        dst[...] = src[...].astype(_BF16)


def _mixer_kernel(*refs, n_cast):
    x_ref, g_ref, win_ref, conv_ref, wout_ref = refs[:5]
    cast_src = refs[5:5 + n_cast]
    o_ref = refs[5 + n_cast]
    cast_dst = refs[6 + n_cast:6 + 2 * n_cast]
    carry_ref, win_bf, wout_bf = refs[6 + 2 * n_cast:]
    d = x_ref.shape[2]
    ts = x_ref.shape[1]

    @pl.when((pl.program_id(0) == 0) & (pl.program_id(1) == 0))
    def _():
        def cast_rows(i, _):
            r = pl.ds(pl.multiple_of(i * CAST_ROWS, CAST_ROWS), CAST_ROWS)
            win_bf[r, :] = win_ref[r, :].astype(_BF16)
            wout_bf[r, :] = wout_ref[r, :].astype(_BF16)
            return 0
        lax.fori_loop(0, d // CAST_ROWS, cast_rows, 0)

    @pl.when(pl.program_id(1) == 0)
    def _():
        carry_ref[...] = jnp.zeros_like(carry_ref)

    _run_casts(cast_src, cast_dst)

    x = x_ref[0]
    h = _rms(x, g_ref[...]).astype(_BF16)
    b_gate = _dot(h, win_bf[:, 0:d])
    c_gate = _dot(h, win_bf[:, d:2 * d])
    v = _dot(h, win_bf[:, 2 * d:3 * d])
    u = c_gate * v

    rows = lax.broadcasted_iota(jnp.int32, u.shape, 0)
    tail = carry_ref[...]
    prev1 = tail[V7X_SUBLANES - 1:V7X_SUBLANES]
    prev2 = tail[V7X_SUBLANES - 2:V7X_SUBLANES - 1]
    u1 = jnp.where(rows == 0, prev1, pltpu.roll(u, 1, 0))
    u2 = jnp.where(rows == 0, prev2,
                   jnp.where(rows == 1, prev1, pltpu.roll(u, 2, 0)))
    w = conv_ref[...]
    y = w[0:1] * u2 + w[1:2] * u1 + w[2:3] * u
    carry_ref[...] = u[ts - V7X_SUBLANES:ts]

    o_ref[0] = x + _dot((b_gate * y).astype(_BF16), wout_bf[...])


def _mixer(x, g, w_in, conv_w, w_out, layer, next_weights):
    b, s, d = x.shape
    ts = BIG_ROW_TILE
    assert s % ts == 0 and d % CAST_ROWS == 0
    n_seq = s // ts
    tok = pl.BlockSpec((1, ts, d), lambda i, j: (i, j, 0))
    cast_in, cast_out, cast_shapes = _cast_jobs(next_weights, b * n_seq,
                                                lambda i, j: i * n_seq + j)
    outs = pl.pallas_call(
        functools.partial(_mixer_kernel, n_cast=len(cast_in)),
        grid=(b, n_seq),
        in_specs=[tok, _const_spec((1, d)), _layer_spec(w_in, layer),
                  _const_spec((CONV_WIDTH, d)), _layer_spec(w_out, layer),
                  *cast_in],
        out_specs=[tok, *cast_out],
        out_shape=[jax.ShapeDtypeStruct(x.shape, _F32), *cast_shapes],
        scratch_shapes=[pltpu.VMEM((V7X_SUBLANES, d), _F32),
                        pltpu.VMEM(w_in.shape[1:], _BF16),
                        pltpu.VMEM(w_out.shape[1:], _BF16)],
        compiler_params=pltpu.CompilerParams(
            dimension_semantics=("arbitrary", "arbitrary"),
            vmem_limit_bytes=V7X_VMEM_LIMIT_BYTES),
        name="mixer",
    )(x, g, w_in, conv_w, w_out, *[w for w, _ in next_weights])
    return outs[0], outs[1:]


def _mlp_qkv_kernel(*refs, n_cast):
    (x_ref, gm_ref, wup_ref, wdn_ref, gkv_ref, wkv_ref, gk_ref, gb_ref, wq_ref,
     gq_ref, ones_ref) = refs[:11]
    cast_src = refs[11:11 + n_cast]
    x1_ref, q_ref, k_ref, v_ref = refs[11 + n_cast:15 + n_cast]
    cast_dst = refs[15 + n_cast:]

    _run_casts(cast_src, cast_dst)

    x1 = _mlp(x_ref[...], gm_ref[...], wup_ref, wdn_ref)
    x1_ref[...] = x1

    nk = k_ref.shape[1]
    hkv = _rms(x1, gkv_ref[...]).astype(_BF16)
    k = _dot(hkv, wkv_ref[:, 0:nk])
    k_ref[...] = _head_rms(k, ones_ref, gk_ref[...]).astype(_BF16)
    v_ref[...] = _dot(hkv, wkv_ref[:, nk:]).astype(_BF16)

    hq = _rms(x1, gb_ref[...]).astype(_BF16)
    q = _head_rms(_dot(hq, wq_ref[...]), ones_ref, gq_ref[...])
    q_ref[...] = (q * (HEAD_DIM ** -0.5 * _LOG2_E)).astype(_BF16)


def _mlp_qkv(x, g_mlp, w_up, w_down, g_kv, w_kv, g_k, g_b, w_q, g_q, ones,
             next_weights):
    m, d = x.shape
    d_ff = w_up.shape[2]
    nq = w_q.shape[2]
    nkv = w_kv.shape[2]
    nk = nq
    tm = ROW_TILE
    assert m % tm == 0 and d_ff % FF_CHUNK == 0

    def rows(n):
        return pl.BlockSpec((tm, n), lambda i: (i, 0))

    cast_in, cast_out, cast_shapes = _cast_jobs(next_weights, m // tm,
                                                lambda i: i)
    outs = pl.pallas_call(
        functools.partial(_mlp_qkv_kernel, n_cast=len(cast_in)),
        grid=(m // tm,),
        in_specs=[rows(d), _const_spec((1, d)), _layer_spec(w_up, 0),
                  _layer_spec(w_down, 0), _const_spec((1, d)),
                  _layer_spec(w_kv, 0), _const_spec((1, nk)),
                  _const_spec((1, d)), _layer_spec(w_q, 0),
                  _const_spec((1, nq)),
                  _const_spec((V7X_MXU_DIM, V7X_MXU_DIM)), *cast_in],
        out_specs=[rows(d), rows(nq), rows(nk), rows(nkv - nk), *cast_out],
        out_shape=[jax.ShapeDtypeStruct((m, d), _F32),
                   jax.ShapeDtypeStruct((m, nq), _BF16),
                   jax.ShapeDtypeStruct((m, nk), _BF16),
                   jax.ShapeDtypeStruct((m, nkv - nk), _BF16), *cast_shapes],
        compiler_params=pltpu.CompilerParams(
            dimension_semantics=("arbitrary",),
            vmem_limit_bytes=V7X_VMEM_LIMIT_BYTES),
        name="mlp_qkv",
    )(x, g_mlp, w_up, w_down, g_kv, w_kv, g_k, g_b, w_q, g_q, ones,
      *[w for w, _ in next_weights])
    return outs[:4], outs[4:]


def _attn_kernel(q_ref, k_ref, v_ref, lq1_ref, lk1_ref, lq2_ref, lk2_ref,
                 sub_ref, o_ref, s_scr, p_scr, vx_scr, *, lam_init):
    s_len = q_ref.shape[1]
    n_heads = q_ref.shape[2] // V_DIM
    tq, tk = Q_TILE, K_TILE
    lam = (jnp.exp(jnp.sum(lq1_ref[...] * lk1_ref[...], keepdims=True))
           - jnp.exp(jnp.sum(lq2_ref[...] * lk2_ref[...], keepdims=True))
           + lam_init)
    out_gain = sub_ref[...] * (1.0 - lam_init)

    for h in range(n_heads):
        vx_scr[h, :, 0:V_DIM] = v_ref[0, :, h * V_DIM:(h + 1) * V_DIM]
        vx_scr[h, :, V_DIM:2 * V_DIM] = jnp.ones((s_len, V_DIM), _BF16)

    lane = lax.broadcasted_iota(jnp.int32, (tq, V_DIM), 1)
    row_id = lax.broadcasted_iota(jnp.int32, (tq, tk), 0)
    row_id = jnp.concatenate([row_id, row_id], axis=0)
    col_id = lax.broadcasted_iota(jnp.int32, (2 * tq, tk), 1)

    def lane_tiles(a):
        return [a[:, t:t + V7X_LANES] for t in range(0, a.shape[1], V7X_LANES)]

    tiles = [(h, qi) for h in range(n_heads)
             for qi in reversed(range(s_len // tq))]
    row_max = {}

    def scores(pos):
        (h, qi), slot = tiles[pos], pos % 2
        q_lo, cols = qi * tq, slice(h * V_DIM, (h + 1) * V_DIM)
        q = q_ref[0, q_lo:q_lo + tq, cols]
        zero = jnp.zeros_like(q)
        q2 = jnp.concatenate([jnp.where(lane < HEAD_DIM, q, zero),
                              jnp.where(lane >= HEAD_DIM, q, zero)], axis=0)
        m_run = jnp.full((2 * tq, V7X_LANES), NEG_INF, _F32)
        for k_lo in range(0, q_lo + tq, tk):
            s = _dot_nt(q2, k_ref[0, k_lo:k_lo + tk, cols])
            if k_lo + tk > q_lo:
                visible = (k_lo + col_id) <= (q_lo + row_id)
                s = jnp.where(visible, s, NEG_INF)
            s_scr[slot, :, k_lo:k_lo + tk] = s
            for part in lane_tiles(s):
                m_run = jnp.maximum(m_run, part)
            yield
        row_max[pos] = jnp.max(m_run, axis=-1, keepdims=True)

    def probs(pos):
        (_, qi), slot = tiles[pos], pos % 2
        for k_lo in range(0, qi * tq + tq, tk):
            p = jnp.exp2(s_scr[slot, :, k_lo:k_lo + tk] - row_max[pos])
            p_scr[slot, :, k_lo:k_lo + tk] = p.astype(_BF16)
            yield

    def values(pos):
        (h, qi), slot = tiles[pos], pos % 2
        q_lo, n_keys = qi * tq, qi * tq + tq
        pv = _dot(p_scr[slot, :, 0:n_keys], vx_scr[h, 0:n_keys, :])
        maps = pv[:, 0:V_DIM] / pv[:, V_DIM:2 * V_DIM]
        o = maps[0:tq] - lam * maps[tq:2 * tq]
        o_ref[0, q_lo:q_lo + tq, h * V_DIM:(h + 1) * V_DIM] = (
            _rms(o, out_gain).astype(_BF16))

    for _ in scores(0):
        pass
    for pos in range(len(tiles) + 1):
        if pos >= 1:
            values(pos - 1)
        active = [stage(p) for stage, p in ((probs, pos), (scores, pos + 1))
                  if p < len(tiles)]
        while active:
            for gen in list(active):
                if next(gen, StopIteration) is StopIteration:
                    active.remove(gen)


def _attention(q, k, v, lq1, lk1, lq2, lk2, sub, lam_init):
    b, s, n = q.shape
    n_heads = n // V_DIM
    assert s % Q_TILE == 0 and s % K_TILE == 0 and n_heads % HEADS_PER_STEP == 0
    head = pl.BlockSpec((1, s, HEADS_PER_STEP * V_DIM), lambda i, h: (i, 0, h))
    vec = _const_spec((1, HEAD_DIM))
    return pl.pallas_call(
        functools.partial(_attn_kernel, lam_init=lam_init),
        grid=(b, n_heads // HEADS_PER_STEP),
        in_specs=[head, head, head, vec, vec, vec, vec, _const_spec((1, V_DIM))],
        out_specs=head,
        out_shape=jax.ShapeDtypeStruct((b, s, n), _BF16),
        scratch_shapes=[pltpu.VMEM((2, 2 * Q_TILE, s), _F32),
                        pltpu.VMEM((2, 2 * Q_TILE, s), _BF16),
                        pltpu.VMEM((HEADS_PER_STEP, s, 2 * V_DIM), _BF16)],
        compiler_params=pltpu.CompilerParams(
            dimension_semantics=("arbitrary", "arbitrary"),
            vmem_limit_bytes=ATTN_VMEM_LIMIT_BYTES),
        name="attention",
    )(q, k, v, lq1, lk1, lq2, lk2, sub)


def _out_mlp_kernel(x_ref, o_ref, wo_ref, gm_ref, wup_ref, wdn_ref, y_ref):
    x2 = x_ref[...] + _dot(o_ref[...], wo_ref[...])
    y_ref[...] = _mlp(x2, gm_ref[...], wup_ref, wdn_ref)


def _out_mlp(x, o, w_o, g_mlp, w_up, w_down):
    m, d = x.shape
    n = o.shape[1]
    d_ff = w_up.shape[2]
    tm = BIG_ROW_TILE
    assert m % tm == 0 and d_ff % FF_CHUNK == 0
    resident = pl.BlockSpec(memory_space=pltpu.VMEM)
    return pl.pallas_call(
        _out_mlp_kernel,
        grid=(m // tm,),
        in_specs=[pl.BlockSpec((tm, d), lambda i: (i, 0)),
                  pl.BlockSpec((tm, n), lambda i: (i, 0)),
                  resident, _const_spec((1, d)), resident, resident],
        out_specs=pl.BlockSpec((tm, d), lambda i: (i, 0)),
        out_shape=jax.ShapeDtypeStruct((m, d), _F32),
        compiler_params=pltpu.CompilerParams(
            dimension_semantics=("arbitrary",),
            vmem_limit_bytes=OUT_MLP_VMEM_LIMIT_BYTES),
        name="out_mlp",
    )(x, o, w_o.reshape(n, d), g_mlp, w_up.reshape(d, d_ff),
      w_down.reshape(d_ff, d))


def _lambda_init(layer_idx):
    return 0.8 - 0.6 * math.exp(-0.3 * layer_idx)


def kernel(x, a_norm, a_w_in, a_conv, a_w_out, kv_norm, w_kv, k_norm, b_norm,
           w_q, q_norm, lam_q1, lam_k1, lam_q2, lam_k2, sub_norm, w_o,
           mlp_norm, w_up, w_down):
    b, s, d = x.shape
    n_a, n_b = a_norm.shape[0], b_norm.shape[0]
    assert n_a == 1 and n_b == 1, "one mixer layer followed by one attention layer"
    n_qk = w_q.shape[2]
    assert w_kv.shape[1] == 2 * n_qk and n_qk % V7X_MXU_DIM == 0

    row = lambda g: g.reshape(1, -1)
    groups = n_qk // HEAD_DIM
    gid = jnp.arange(V7X_MXU_DIM) // HEAD_DIM
    ones = (gid[:, None] == gid[None, :]).astype(_BF16)

    x, (w_up_bf0, w_down_bf0, w_kv_bf, w_q_bf) = _mixer(
        x, row(a_norm[0]), a_w_in, a_conv[0], a_w_out, 0,
        [(w_up, 0), (w_down, 0), (w_kv[None], 0), (w_q, 0)])
    (x1, q, k, v), (w_o_bf, w_up_bf1, w_down_bf1) = _mlp_qkv(
        x.reshape(b * s, d), row(mlp_norm[0]), w_up_bf0, w_down_bf0,
        row(kv_norm), w_kv_bf, row(jnp.tile(k_norm, groups)),
        row(b_norm[0]), w_q_bf, row(jnp.tile(q_norm[0], groups)), ones,
        [(w_o, 0), (w_up, 1), (w_down, 1)])
    o = _attention(q.reshape(b, s, n_qk), k.reshape(b, s, n_qk),
                   v.reshape(b, s, -1), row(lam_q1[0]), row(lam_k1[0]),
                   row(lam_q2[0]), row(lam_k2[0]), row(sub_norm[0]),
                   _lambda_init(n_a))
    y = _out_mlp(x1, o.reshape(b * s, -1), w_o_bf, row(mlp_norm[1]),
                 w_up_bf1, w_down_bf1)
    return y.reshape(b, s, d)
```

```python
import functools
import math

import jax
import jax.numpy as jnp
import numpy as np
from jax import lax
from jax.experimental import pallas as pl
from jax.experimental.pallas import tpu as pltpu

EPS = 1e-6
NEG_INF = -1e30
HEAD_DIM = 64
V_DIM = 2 * HEAD_DIM
CONV_WIDTH = 3

V7X_LANES = 128
V7X_SUBLANES = 8
V7X_MXU_DIM = 256
V7X_BF16_ROWS_PER_VREG = 16
V7X_VMEM_LIMIT_BYTES = 60 * 1024 * 1024
ATTN_VMEM_LIMIT_BYTES = 36 * 1024 * 1024
OUT_MLP_VMEM_LIMIT_BYTES = 34 * 1024 * 1024

ROW_TILE = 512
BIG_ROW_TILE = 1024
FF_CHUNK = 1024
Q_TILE = 256
K_TILE = 256
HEADS_PER_STEP = 4
CAST_ROWS = 128

_LOG2_E = math.log2(math.e)
_BF16 = jnp.bfloat16
_F32 = jnp.float32


def _dot(a, b):
    return jnp.dot(a, b, preferred_element_type=_F32)


def _dot_nt(a, b):
    return lax.dot_general(a, b, (((1,), (1,)), ((), ())),
                           preferred_element_type=_F32)


def _rms(x, g):
    ms = jnp.mean(x * x, axis=-1, keepdims=True)
    return x * lax.rsqrt(ms + EPS) * g


def _mlp(x, g, wup_ref, wdn_ref):
    h = _rms(x, g).astype(_BF16)
    acc = x
    d_ff = wup_ref.shape[1]
    for c in range(d_ff // FF_CHUNK):
        sl = slice(c * FF_CHUNK, (c + 1) * FF_CHUNK)
        a = jnp.maximum(_dot(h, wup_ref[:, sl]), 0.0)
        acc = acc + _dot((a * a).astype(_BF16), wdn_ref[sl, :])
    return acc


def _head_rms(y, ones_ref, gain):
    y2 = (y * y).astype(_BF16)
    n = y.shape[1]
    parts = [_dot(y2[:, t:t + V7X_MXU_DIM], ones_ref[...])
             for t in range(0, n, V7X_MXU_DIM)]
    ss = jnp.concatenate(parts, axis=1)
    gain_row = jnp.concatenate([gain] * (n // HEAD_DIM), axis=1)
    return y * lax.rsqrt(ss * (1.0 / HEAD_DIM) + EPS) * gain_row


def _const_spec(shape):
    return pl.BlockSpec(shape, lambda *_: (0,) * len(shape),
                        pipeline_mode=pl.Buffered(1))


def _layer_spec(w, layer):
    _, r, c = w.shape
    return pl.BlockSpec((None, r, c), lambda *_: (layer, 0, 0),
                        pipeline_mode=pl.Buffered(1))


def _cast_jobs(weights, n_steps, step_of):
    in_specs, out_specs, shapes = [], [], []
    for w, layer in weights:
        _, r, c = w.shape
        assert r % (n_steps * V7X_BF16_ROWS_PER_VREG) == 0
        block = (None, r // n_steps, c)
        in_specs.append(pl.BlockSpec(
            block, lambda *ids, layer=layer: (layer, step_of(*ids), 0)))
        out_specs.append(pl.BlockSpec(
            block, lambda *ids: (0, step_of(*ids), 0)))
        shapes.append(jax.ShapeDtypeStruct((1, r, c), _BF16))
    return in_specs, out_specs, shapes


def _run_casts(src_refs, dst_refs):
    for src, dst in zip(src_refs, dst_refs):
        dst[...] = src[...].astype(_BF16)


def _mixer_kernel(*refs, n_cast):
    x_ref, g_ref, win_ref, conv_ref, wout_ref = refs[:5]
    cast_src = refs[5:5 + n_cast]
    o_ref = refs[5 + n_cast]
    cast_dst = refs[6 + n_cast:6 + 2 * n_cast]
    carry_ref, win_bf, wout_bf = refs[6 + 2 * n_cast:]
    d = x_ref.shape[2]
    ts = x_ref.shape[1]

    @pl.when((pl.program_id(0) == 0) & (pl.program_id(1) == 0))
    def _():
        def cast_rows(i, _):
            r = pl.ds(pl.multiple_of(i * CAST_ROWS, CAST_ROWS), CAST_ROWS)
            win_bf[r, :] = win_ref[r, :].astype(_BF16)
            wout_bf[r, :] = wout_ref[r, :].astype(_BF16)
            return 0
        lax.fori_loop(0, d // CAST_ROWS, cast_rows, 0)

    @pl.when(pl.program_id(1) == 0)
    def _():
        carry_ref[...] = jnp.zeros_like(carry_ref)

    _run_casts(cast_src, cast_dst)

    x = x_ref[0]
    h = _rms(x, g_ref[...]).astype(_BF16)
    b_gate = _dot(h, win_bf[:, 0:d])
    c_gate = _dot(h, win_bf[:, d:2 * d])
    v = _dot(h, win_bf[:, 2 * d:3 * d])
    u = c_gate * v

    rows = lax.broadcasted_iota(jnp.int32, u.shape, 0)
    tail = carry_ref[...]
    prev1 = tail[V7X_SUBLANES - 1:V7X_SUBLANES]
    prev2 = tail[V7X_SUBLANES - 2:V7X_SUBLANES - 1]
    u1 = jnp.where(rows == 0, prev1, pltpu.roll(u, 1, 0))
    u2 = jnp.where(rows == 0, prev2,
                   jnp.where(rows == 1, prev1, pltpu.roll(u, 2, 0)))
    w = conv_ref[...]
    y = w[:, 0:d] * u2 + w[:, d:2 * d] * u1 + w[:, 2 * d:3 * d] * u
    carry_ref[...] = u[ts - V7X_SUBLANES:ts]

    o_ref[0] = x + _dot((b_gate * y).astype(_BF16), wout_bf[...])


def _mixer(x, g, w_in, conv_w, w_out, layer, next_weights):
    b, s, d = x.shape
    ts = BIG_ROW_TILE
    assert s % ts == 0 and d % CAST_ROWS == 0
    n_seq = s // ts
    tok = pl.BlockSpec((1, ts, d), lambda i, j: (i, j, 0))
    cast_in, cast_out, cast_shapes = _cast_jobs(next_weights, b * n_seq,
                                                lambda i, j: i * n_seq + j)
    outs = pl.pallas_call(
        functools.partial(_mixer_kernel, n_cast=len(cast_in)),
        grid=(b, n_seq),
        in_specs=[tok, _const_spec((1, d)), _layer_spec(w_in, layer),
                  _const_spec((1, CONV_WIDTH * d)), _layer_spec(w_out, layer),
                  *cast_in],
        out_specs=[tok, *cast_out],
        out_shape=[jax.ShapeDtypeStruct(x.shape, _F32), *cast_shapes],
        scratch_shapes=[pltpu.VMEM((V7X_SUBLANES, d), _F32),
                        pltpu.VMEM(w_in.shape[1:], _BF16),
                        pltpu.VMEM(w_out.shape[1:], _BF16)],
        compiler_params=pltpu.CompilerParams(
            dimension_semantics=("arbitrary", "arbitrary"),
            vmem_limit_bytes=V7X_VMEM_LIMIT_BYTES),
        name="mixer",
    )(x, g, w_in, conv_w, w_out, *[w for w, _ in next_weights])
    return outs[0], outs[1:]


def _mlp_qkv_kernel(*refs, n_cast, mlp_layer):
    (x_ref, gm_ref, wup_ref, wdn_ref, gkv_ref, wkv_ref, gk_ref, gb_ref, wq_ref,
     gq_ref, ones_ref) = refs[:11]
    cast_src = refs[11:11 + n_cast]
    x1_ref, q_ref, k_ref, v_ref = refs[11 + n_cast:15 + n_cast]
    cast_dst = refs[15 + n_cast:]

    _run_casts(cast_src, cast_dst)

    x1 = _mlp(x_ref[...], gm_ref[mlp_layer:mlp_layer + 1, :], wup_ref, wdn_ref)
    x1_ref[...] = x1

    nk = k_ref.shape[1]
    hkv = _rms(x1, gkv_ref[...]).astype(_BF16)
    k = _dot(hkv, wkv_ref[:, 0:nk])
    k_ref[...] = _head_rms(k, ones_ref, gk_ref[...]).astype(_BF16)
    v_ref[...] = _dot(hkv, wkv_ref[:, nk:]).astype(_BF16)

    hq = _rms(x1, gb_ref[...]).astype(_BF16)
    q = _head_rms(_dot(hq, wq_ref[...]), ones_ref, gq_ref[...])
    q_ref[...] = (q * (HEAD_DIM ** -0.5 * _LOG2_E)).astype(_BF16)


def _mlp_qkv(x, g_mlp, mlp_layer, w_up, w_down, g_kv, w_kv, g_k, g_b, w_q, g_q,
             ones, next_weights):
    m, d = x.shape
    d_ff = w_up.shape[2]
    nq = w_q.shape[2]
    nkv = w_kv.shape[2]
    nk = nq
    tm = ROW_TILE
    assert m % tm == 0 and d_ff % FF_CHUNK == 0

    def rows(n):
        return pl.BlockSpec((tm, n), lambda i: (i, 0))

    cast_in, cast_out, cast_shapes = _cast_jobs(next_weights, m // tm,
                                                lambda i: i)
    outs = pl.pallas_call(
        functools.partial(_mlp_qkv_kernel, n_cast=len(cast_in),
                          mlp_layer=mlp_layer),
        grid=(m // tm,),
        in_specs=[rows(d), _const_spec(g_mlp.shape), _layer_spec(w_up, 0),
                  _layer_spec(w_down, 0), _const_spec((1, d)),
                  _layer_spec(w_kv, 0), _const_spec((1, HEAD_DIM)),
                  _const_spec((1, d)), _layer_spec(w_q, 0),
                  _const_spec((1, HEAD_DIM)),
                  _const_spec((V7X_MXU_DIM, V7X_MXU_DIM)), *cast_in],
        out_specs=[rows(d), rows(nq), rows(nk), rows(nkv - nk), *cast_out],
        out_shape=[jax.ShapeDtypeStruct((m, d), _F32),
                   jax.ShapeDtypeStruct((m, nq), _BF16),
                   jax.ShapeDtypeStruct((m, nk), _BF16),
                   jax.ShapeDtypeStruct((m, nkv - nk), _BF16), *cast_shapes],
        compiler_params=pltpu.CompilerParams(
            dimension_semantics=("arbitrary",),
            vmem_limit_bytes=V7X_VMEM_LIMIT_BYTES),
        name="mlp_qkv",
    )(x, g_mlp, w_up, w_down, g_kv, w_kv, g_k, g_b, w_q, g_q, ones,
      *[w for w, _ in next_weights])
    return outs[:4], outs[4:]


def _attn_kernel(q_ref, k_ref, v_ref, lq1_ref, lk1_ref, lq2_ref, lk2_ref,
                 sub_ref, o_ref, s_scr, p_scr, vx_scr, *, lam_init):
    s_len = q_ref.shape[1]
    n_heads = q_ref.shape[2] // V_DIM
    tq, tk = Q_TILE, K_TILE
    lam = (jnp.exp(jnp.sum(lq1_ref[...] * lk1_ref[...], keepdims=True))
           - jnp.exp(jnp.sum(lq2_ref[...] * lk2_ref[...], keepdims=True))
           + lam_init)
    out_gain = sub_ref[...] * (1.0 - lam_init)

    for h in range(n_heads):
        vx_scr[h, :, 0:V_DIM] = v_ref[0, :, h * V_DIM:(h + 1) * V_DIM]
        vx_scr[h, :, V_DIM:2 * V_DIM] = jnp.ones((s_len, V_DIM), _BF16)

    lane = lax.broadcasted_iota(jnp.int32, (tq, V_DIM), 1)
    row_id = lax.broadcasted_iota(jnp.int32, (tq, tk), 0)
    row_id = jnp.concatenate([row_id, row_id], axis=0)
    col_id = lax.broadcasted_iota(jnp.int32, (2 * tq, tk), 1)

    def lane_tiles(a):
        return [a[:, t:t + V7X_LANES] for t in range(0, a.shape[1], V7X_LANES)]

    tiles = [(h, qi) for h in range(n_heads)
             for qi in reversed(range(s_len // tq))]
    row_max = {}

    def scores(pos):
        (h, qi), slot = tiles[pos], pos % 2
        q_lo, cols = qi * tq, slice(h * V_DIM, (h + 1) * V_DIM)
        q = q_ref[0, q_lo:q_lo + tq, cols]
        zero = jnp.zeros_like(q)
        q2 = jnp.concatenate([jnp.where(lane < HEAD_DIM, q, zero),
                              jnp.where(lane >= HEAD_DIM, q, zero)], axis=0)
        m_run = jnp.full((2 * tq, V7X_LANES), NEG_INF, _F32)
        for k_lo in range(0, q_lo + tq, tk):
            s = _dot_nt(q2, k_ref[0, k_lo:k_lo + tk, cols])
            if k_lo + tk > q_lo:
                visible = (k_lo + col_id) <= (q_lo + row_id)
                s = jnp.where(visible, s, NEG_INF)
            s_scr[slot, :, k_lo:k_lo + tk] = s
            for part in lane_tiles(s):
                m_run = jnp.maximum(m_run, part)
            yield
        row_max[pos] = jnp.max(m_run, axis=-1, keepdims=True)

    def probs(pos):
        (_, qi), slot = tiles[pos], pos % 2
        for k_lo in range(0, qi * tq + tq, tk):
            p = jnp.exp2(s_scr[slot, :, k_lo:k_lo + tk] - row_max[pos])
            p_scr[slot, :, k_lo:k_lo + tk] = p.astype(_BF16)
            yield

    def values(pos):
        (h, qi), slot = tiles[pos], pos % 2
        q_lo, n_keys = qi * tq, qi * tq + tq
        pv = _dot(p_scr[slot, :, 0:n_keys], vx_scr[h, 0:n_keys, :])
        maps = pv[:, 0:V_DIM] / pv[:, V_DIM:2 * V_DIM]
        o = maps[0:tq] - lam * maps[tq:2 * tq]
        o_ref[0, q_lo:q_lo + tq, h * V_DIM:(h + 1) * V_DIM] = (
            _rms(o, out_gain).astype(_BF16))

    for _ in scores(0):
        pass
    for pos in range(len(tiles) + 1):
        if pos >= 1:
            values(pos - 1)
        active = [stage(p) for stage, p in ((probs, pos), (scores, pos + 1))
                  if p < len(tiles)]
        while active:
            for gen in list(active):
                if next(gen, StopIteration) is StopIteration:
                    active.remove(gen)


def _attention(q, k, v, lq1, lk1, lq2, lk2, sub, lam_init):
    b, s, n = q.shape
    n_heads = n // V_DIM
    assert s % Q_TILE == 0 and s % K_TILE == 0 and n_heads % HEADS_PER_STEP == 0
    head = pl.BlockSpec((1, s, HEADS_PER_STEP * V_DIM), lambda i, h: (i, 0, h))
    vec = _const_spec((1, HEAD_DIM))
    return pl.pallas_call(
        functools.partial(_attn_kernel, lam_init=lam_init),
        grid=(b, n_heads // HEADS_PER_STEP),
        in_specs=[head, head, head, vec, vec, vec, vec, _const_spec((1, V_DIM))],
        out_specs=head,
        out_shape=jax.ShapeDtypeStruct((b, s, n), _BF16),
        scratch_shapes=[pltpu.VMEM((2, 2 * Q_TILE, s), _F32),
                        pltpu.VMEM((2, 2 * Q_TILE, s), _BF16),
                        pltpu.VMEM((HEADS_PER_STEP, s, 2 * V_DIM), _BF16)],
        compiler_params=pltpu.CompilerParams(
            dimension_semantics=("arbitrary", "arbitrary"),
            vmem_limit_bytes=ATTN_VMEM_LIMIT_BYTES),
        name="attention",
    )(q, k, v, lq1, lk1, lq2, lk2, sub)


def _out_mlp_kernel(x_ref, o_ref, wo_ref, gm_ref, wup_ref, wdn_ref, y_ref, *,
                    mlp_layer):
    x2 = x_ref[...] + _dot(o_ref[...], wo_ref[...])
    y_ref[...] = _mlp(x2, gm_ref[mlp_layer:mlp_layer + 1, :], wup_ref, wdn_ref)


def _out_mlp(x, o, w_o, g_mlp, mlp_layer, w_up, w_down):
    m, d = x.shape
    n = o.shape[1]
    d_ff = w_up.shape[2]
    tm = BIG_ROW_TILE
    assert m % tm == 0 and d_ff % FF_CHUNK == 0
    resident = pl.BlockSpec(memory_space=pltpu.VMEM)
    return pl.pallas_call(
        functools.partial(_out_mlp_kernel, mlp_layer=mlp_layer),
        grid=(m // tm,),
        in_specs=[pl.BlockSpec((tm, d), lambda i: (i, 0)),
                  pl.BlockSpec((tm, n), lambda i: (i, 0)),
                  resident, _const_spec(g_mlp.shape), resident, resident],
        out_specs=pl.BlockSpec((tm, d), lambda i: (i, 0)),
        out_shape=jax.ShapeDtypeStruct((m, d), _F32),
        compiler_params=pltpu.CompilerParams(
            dimension_semantics=("arbitrary",),
            vmem_limit_bytes=OUT_MLP_VMEM_LIMIT_BYTES),
        name="out_mlp",
    )(x, o, w_o.reshape(n, d), g_mlp, w_up.reshape(d, d_ff),
      w_down.reshape(d_ff, d))


def _lambda_init(layer_idx):
    return 0.8 - 0.6 * math.exp(-0.3 * layer_idx)


def kernel(x, a_norm, a_w_in, a_conv, a_w_out, kv_norm, w_kv, k_norm, b_norm,
           w_q, q_norm, lam_q1, lam_k1, lam_q2, lam_k2, sub_norm, w_o,
           mlp_norm, w_up, w_down):
    b, s, d = x.shape
    n_a, n_b = a_norm.shape[0], b_norm.shape[0]
    assert n_a == 1 and n_b == 1, "one mixer layer followed by one attention layer"
    n_qk = w_q.shape[2]
    assert w_kv.shape[1] == 2 * n_qk and n_qk % V7X_MXU_DIM == 0

    row = lambda g: g.reshape(1, -1)
    gid = np.arange(V7X_MXU_DIM) // HEAD_DIM
    ones = jnp.asarray(gid[:, None] == gid[None, :], dtype=_BF16)

    x, (w_up_bf0, w_down_bf0, w_kv_bf, w_q_bf) = _mixer(
        x, row(a_norm[0]), a_w_in, row(a_conv[0]), a_w_out, 0,
        [(w_up, 0), (w_down, 0), (w_kv[None], 0), (w_q, 0)])
    (x1, q, k, v), (w_o_bf, w_up_bf1, w_down_bf1) = _mlp_qkv(
        x.reshape(b * s, d), mlp_norm, 0, w_up_bf0, w_down_bf0,
        row(kv_norm), w_kv_bf, row(k_norm), row(b_norm[0]), w_q_bf,
        row(q_norm[0]), ones, [(w_o, 0), (w_up, 1), (w_down, 1)])
    o = _attention(q.reshape(b, s, n_qk), k.reshape(b, s, n_qk),
                   v.reshape(b, s, -1), row(lam_q1[0]), row(lam_k1[0]),
                   row(lam_q2[0]), row(lam_k2[0]), row(sub_norm[0]),
                   _lambda_init(n_a))
    y = _out_mlp(x1, o.reshape(b * s, -1), w_o_bf, mlp_norm, 1,
                 w_up_bf1, w_down_bf1)
    return y.reshape(b, s, d)
```

```python
import functools
import math

import jax
import jax.numpy as jnp
import numpy as np
from jax import lax
from jax.experimental import pallas as pl
from jax.experimental.pallas import tpu as pltpu

EPS = 1e-6
NEG_INF = -1e30
HEAD_DIM = 64
V_DIM = 2 * HEAD_DIM
CONV_WIDTH = 3

V7X_LANES = 128
V7X_SUBLANES = 8
V7X_MXU_DIM = 256
V7X_BF16_ROWS_PER_VREG = 16
V7X_VMEM_LIMIT_BYTES = 60 * 1024 * 1024
ATTN_VMEM_LIMIT_BYTES = 24 * 1024 * 1024
OUT_MLP_VMEM_LIMIT_BYTES = 34 * 1024 * 1024

ROW_TILE = 512
BIG_ROW_TILE = 1024
FF_CHUNK = 1024
Q_TILE = 128
K_TILE = 256
HEADS_PER_STEP = 2
CAST_ROWS = 128

_LOG2_E = math.log2(math.e)
_BF16 = jnp.bfloat16
_F32 = jnp.float32


def _dot(a, b):
    return jnp.dot(a, b, preferred_element_type=_F32)


def _dot_nt(a, b):
    return lax.dot_general(a, b, (((1,), (1,)), ((), ())),
                           preferred_element_type=_F32)


def _rms(x, g):
    ms = jnp.mean(x * x, axis=-1, keepdims=True)
    return x * lax.rsqrt(ms + EPS) * g


def _mlp(x, g, wup_ref, wdn_ref):
    h = _rms(x, g).astype(_BF16)
    acc = x
    d_ff = wup_ref.shape[1]
    for c in range(d_ff // FF_CHUNK):
        sl = slice(c * FF_CHUNK, (c + 1) * FF_CHUNK)
        a = jnp.maximum(_dot(h, wup_ref[:, sl]), 0.0)
        acc = acc + _dot((a * a).astype(_BF16), wdn_ref[sl, :])
    return acc


def _head_rms(y, ones_ref, gain):
    y2 = (y * y).astype(_BF16)
    n = y.shape[1]
    parts = [_dot(y2[:, t:t + V7X_MXU_DIM], ones_ref[...])
             for t in range(0, n, V7X_MXU_DIM)]
    ss = jnp.concatenate(parts, axis=1)
    gain_row = jnp.concatenate([gain] * (n // HEAD_DIM), axis=1)
    return y * lax.rsqrt(ss * (1.0 / HEAD_DIM) + EPS) * gain_row


def _const_spec(shape):
    return pl.BlockSpec(shape, lambda *_: (0,) * len(shape),
                        pipeline_mode=pl.Buffered(1))


def _layer_spec(w, layer):
    _, r, c = w.shape
    return pl.BlockSpec((None, r, c), lambda *_: (layer, 0, 0),
                        pipeline_mode=pl.Buffered(1))


def _cast_jobs(weights, n_steps, step_of):
    in_specs, out_specs, shapes = [], [], []
    for w, layer in weights:
        _, r, c = w.shape
        assert r % (n_steps * V7X_BF16_ROWS_PER_VREG) == 0
        block = (None, r // n_steps, c)
        in_specs.append(pl.BlockSpec(
            block, lambda *ids, layer=layer: (layer, step_of(*ids), 0)))
        out_specs.append(pl.BlockSpec(
            block, lambda *ids: (0, step_of(*ids), 0)))
        shapes.append(jax.ShapeDtypeStruct((1, r, c), _BF16))
    return in_specs, out_specs, shapes


def _run_casts(src_refs, dst_refs):
    for src, dst in zip(src_refs, dst_refs):
        dst[...] = src[...].astype(_BF16)


def _mixer_kernel(*refs, n_cast):
    x_ref, g_ref, win_ref, conv_ref, wout_ref = refs[:5]
    cast_src = refs[5:5 + n_cast]
    o_ref = refs[5 + n_cast]
    cast_dst = refs[6 + n_cast:6 + 2 * n_cast]
    carry_ref, win_bf, wout_bf = refs[6 + 2 * n_cast:]
    d = x_ref.shape[2]
    ts = x_ref.shape[1]

    @pl.when((pl.program_id(0) == 0) & (pl.program_id(1) == 0))
    def _():
        def cast_rows(i, _):
            r = pl.ds(pl.multiple_of(i * CAST_ROWS, CAST_ROWS), CAST_ROWS)
            win_bf[r, :] = win_ref[r, :].astype(_BF16)
            wout_bf[r, :] = wout_ref[r, :].astype(_BF16)
            return 0
        lax.fori_loop(0, d // CAST_ROWS, cast_rows, 0)

    @pl.when(pl.program_id(1) == 0)
    def _():
        carry_ref[...] = jnp.zeros_like(carry_ref)

    _run_casts(cast_src, cast_dst)

    x = x_ref[0]
    h = _rms(x, g_ref[...]).astype(_BF16)
    b_gate = _dot(h, win_bf[:, 0:d])
    c_gate = _dot(h, win_bf[:, d:2 * d])
    v = _dot(h, win_bf[:, 2 * d:3 * d])
    u = c_gate * v

    rows = lax.broadcasted_iota(jnp.int32, u.shape, 0)
    tail = carry_ref[...]
    prev1 = tail[V7X_SUBLANES - 1:V7X_SUBLANES]
    prev2 = tail[V7X_SUBLANES - 2:V7X_SUBLANES - 1]
    u1 = jnp.where(rows == 0, prev1, pltpu.roll(u, 1, 0))
    u2 = jnp.where(rows == 0, prev2,
                   jnp.where(rows == 1, prev1, pltpu.roll(u, 2, 0)))
    w = conv_ref[...]
    y = w[:, 0:d] * u2 + w[:, d:2 * d] * u1 + w[:, 2 * d:3 * d] * u
    carry_ref[...] = u[ts - V7X_SUBLANES:ts]

    o_ref[0] = x + _dot((b_gate * y).astype(_BF16), wout_bf[...])


def _mixer(x, g, w_in, conv_w, w_out, layer, next_weights):
    b, s, d = x.shape
    ts = BIG_ROW_TILE
    assert s % ts == 0 and d % CAST_ROWS == 0
    n_seq = s // ts
    tok = pl.BlockSpec((1, ts, d), lambda i, j: (i, j, 0))
    cast_in, cast_out, cast_shapes = _cast_jobs(next_weights, b * n_seq,
                                                lambda i, j: i * n_seq + j)
    outs = pl.pallas_call(
        functools.partial(_mixer_kernel, n_cast=len(cast_in)),
        grid=(b, n_seq),
        in_specs=[tok, _const_spec((1, d)), _layer_spec(w_in, layer),
                  _const_spec((1, CONV_WIDTH * d)), _layer_spec(w_out, layer),
                  *cast_in],
        out_specs=[tok, *cast_out],
        out_shape=[jax.ShapeDtypeStruct(x.shape, _F32), *cast_shapes],
        scratch_shapes=[pltpu.VMEM((V7X_SUBLANES, d), _F32),
                        pltpu.VMEM(w_in.shape[1:], _BF16),
                        pltpu.VMEM(w_out.shape[1:], _BF16)],
        compiler_params=pltpu.CompilerParams(
            dimension_semantics=("arbitrary", "arbitrary"),
            vmem_limit_bytes=V7X_VMEM_LIMIT_BYTES),
        name="mixer",
    )(x, g, w_in, conv_w, w_out, *[w for w, _ in next_weights])
    return outs[0], outs[1:]


def _mlp_qkv_kernel(*refs, n_cast, mlp_layer):
    (x_ref, gm_ref, wup_ref, wdn_ref, gkv_ref, wkv_ref, gk_ref, gb_ref, wq_ref,
     gq_ref, ones_ref) = refs[:11]
    cast_src = refs[11:11 + n_cast]
    x1_ref, q_ref, k_ref, v_ref = refs[11 + n_cast:15 + n_cast]
    cast_dst = refs[15 + n_cast:]

    _run_casts(cast_src, cast_dst)

    x1 = _mlp(x_ref[...], gm_ref[mlp_layer:mlp_layer + 1, :], wup_ref, wdn_ref)
    x1_ref[...] = x1

    nk = k_ref.shape[1]
    hkv = _rms(x1, gkv_ref[...]).astype(_BF16)
    k = _dot(hkv, wkv_ref[:, 0:nk])
    k_ref[...] = _head_rms(k, ones_ref, gk_ref[...]).astype(_BF16)
    v_ref[...] = _dot(hkv, wkv_ref[:, nk:]).astype(_BF16)

    hq = _rms(x1, gb_ref[...]).astype(_BF16)
    q = _head_rms(_dot(hq, wq_ref[...]), ones_ref, gq_ref[...])
    q_ref[...] = (q * (HEAD_DIM ** -0.5 * _LOG2_E)).astype(_BF16)


def _mlp_qkv(x, g_mlp, mlp_layer, w_up, w_down, g_kv, w_kv, g_k, g_b, w_q, g_q,
             ones, next_weights):
    m, d = x.shape
    d_ff = w_up.shape[2]
    nq = w_q.shape[2]
    nkv = w_kv.shape[2]
    nk = nq
    tm = ROW_TILE
    assert m % tm == 0 and d_ff % FF_CHUNK == 0

    def rows(n):
        return pl.BlockSpec((tm, n), lambda i: (i, 0))

    cast_in, cast_out, cast_shapes = _cast_jobs(next_weights, m // tm,
                                                lambda i: i)
    outs = pl.pallas_call(
        functools.partial(_mlp_qkv_kernel, n_cast=len(cast_in),
                          mlp_layer=mlp_layer),
        grid=(m // tm,),
        in_specs=[rows(d), _const_spec(g_mlp.shape), _layer_spec(w_up, 0),
                  _layer_spec(w_down, 0), _const_spec((1, d)),
                  _layer_spec(w_kv, 0), _const_spec((1, HEAD_DIM)),
                  _const_spec((1, d)), _layer_spec(w_q, 0),
                  _const_spec((1, HEAD_DIM)),
                  _const_spec((V7X_MXU_DIM, V7X_MXU_DIM)), *cast_in],
        out_specs=[rows(d), rows(nq), rows(nk), rows(nkv - nk), *cast_out],
        out_shape=[jax.ShapeDtypeStruct((m, d), _F32),
                   jax.ShapeDtypeStruct((m, nq), _BF16),
                   jax.ShapeDtypeStruct((m, nk), _BF16),
                   jax.ShapeDtypeStruct((m, nkv - nk), _BF16), *cast_shapes],
        compiler_params=pltpu.CompilerParams(
            dimension_semantics=("arbitrary",),
            vmem_limit_bytes=V7X_VMEM_LIMIT_BYTES),
        name="mlp_qkv",
    )(x, g_mlp, w_up, w_down, g_kv, w_kv, g_k, g_b, w_q, g_q, ones,
      *[w for w, _ in next_weights])
    return outs[:4], outs[4:]


def _attn_kernel(q_ref, k_ref, v_ref, lq1_ref, lk1_ref, lq2_ref, lk2_ref,
                 sub_ref, o_ref, s_scr, p_scr, vx_scr, *, lam_init):
    s_len = q_ref.shape[1]
    n_heads = q_ref.shape[2] // V_DIM
    tq, tk = Q_TILE, K_TILE
    lam = (jnp.exp(jnp.sum(lq1_ref[...] * lk1_ref[...], keepdims=True))
           - jnp.exp(jnp.sum(lq2_ref[...] * lk2_ref[...], keepdims=True))
           + lam_init)
    out_gain = sub_ref[...] * (1.0 - lam_init)

    for h in range(n_heads):
        vx_scr[h, :, 0:V_DIM] = v_ref[0, :, h * V_DIM:(h + 1) * V_DIM]
        vx_scr[h, :, V_DIM:2 * V_DIM] = jnp.ones((s_len, V_DIM), _BF16)

    lane = lax.broadcasted_iota(jnp.int32, (tq, V_DIM), 1)
    row_id = lax.broadcasted_iota(jnp.int32, (tq, tk), 0)
    row_id = jnp.concatenate([row_id, row_id], axis=0)
    col_id = lax.broadcasted_iota(jnp.int32, (2 * tq, tk), 1)

    def lane_tiles(a):
        return [a[:, t:t + V7X_LANES] for t in range(0, a.shape[1], V7X_LANES)]

    tiles = [(h, qi) for h in range(n_heads)
             for qi in reversed(range(s_len // tq))]
    row_max = {}

    def scores(pos):
        (h, qi), slot = tiles[pos], pos % 2
        q_lo, cols = qi * tq, slice(h * V_DIM, (h + 1) * V_DIM)
        q = q_ref[0, q_lo:q_lo + tq, cols]
        zero = jnp.zeros_like(q)
        q2 = jnp.concatenate([jnp.where(lane < HEAD_DIM, q, zero),
                              jnp.where(lane >= HEAD_DIM, q, zero)], axis=0)
        m_run = jnp.full((2 * tq, V7X_LANES), NEG_INF, _F32)
        for k_lo in range(0, q_lo + tq, tk):
            s = _dot_nt(q2, k_ref[0, k_lo:k_lo + tk, cols])
            if k_lo + tk > q_lo:
                visible = (k_lo + col_id) <= (q_lo + row_id)
                s = jnp.where(visible, s, NEG_INF)
            s_scr[slot, :, k_lo:k_lo + tk] = s
            for part in lane_tiles(s):
                m_run = jnp.maximum(m_run, part)
            yield
        row_max[pos] = jnp.max(m_run, axis=-1, keepdims=True)

    def probs(pos):
        (_, qi), slot = tiles[pos], pos % 2
        for k_lo in range(0, qi * tq + tq, tk):
            p = jnp.exp2(s_scr[slot, :, k_lo:k_lo + tk] - row_max[pos])
            p_scr[slot, :, k_lo:k_lo + tk] = p.astype(_BF16)
            yield

    def values(pos):
        (h, qi), slot = tiles[pos], pos % 2
        q_lo, n_keys = qi * tq, qi * tq + tq
        pv = _dot(p_scr[slot, :, 0:n_keys], vx_scr[h, 0:n_keys, :])
        maps = pv[:, 0:V_DIM] / pv[:, V_DIM:2 * V_DIM]
        o = maps[0:tq] - lam * maps[tq:2 * tq]
        o_ref[0, q_lo:q_lo + tq, h * V_DIM:(h + 1) * V_DIM] = (
            _rms(o, out_gain).astype(_BF16))

    for _ in scores(0):
        pass
    for pos in range(len(tiles) + 1):
        if pos >= 1:
            values(pos - 1)
        active = [stage(p) for stage, p in ((probs, pos), (scores, pos + 1))
                  if p < len(tiles)]
        while active:
            for gen in list(active):
                if next(gen, StopIteration) is StopIteration:
                    active.remove(gen)


def _attention(q, k, v, lq1, lk1, lq2, lk2, sub, lam_init):
    b, s, n = q.shape
    n_heads = n // V_DIM
    assert s % Q_TILE == 0 and s % K_TILE == 0 and n_heads % HEADS_PER_STEP == 0
    head = pl.BlockSpec((1, s, HEADS_PER_STEP * V_DIM), lambda i, h: (i, 0, h))
    vec = _const_spec((1, HEAD_DIM))
    return pl.pallas_call(
        functools.partial(_attn_kernel, lam_init=lam_init),
        grid=(b, n_heads // HEADS_PER_STEP),
        in_specs=[head, head, head, vec, vec, vec, vec, _const_spec((1, V_DIM))],
        out_specs=head,
        out_shape=jax.ShapeDtypeStruct((b, s, n), _BF16),
        scratch_shapes=[pltpu.VMEM((2, 2 * Q_TILE, s), _F32),
                        pltpu.VMEM((2, 2 * Q_TILE, s), _BF16),
                        pltpu.VMEM((HEADS_PER_STEP, s, 2 * V_DIM), _BF16)],
        compiler_params=pltpu.CompilerParams(
            dimension_semantics=("arbitrary", "arbitrary"),
            vmem_limit_bytes=ATTN_VMEM_LIMIT_BYTES),
        name="attention",
    )(q, k, v, lq1, lk1, lq2, lk2, sub)


def _out_mlp_kernel(x_ref, o_ref, wo_ref, gm_ref, wup_ref, wdn_ref, y_ref, *,
                    mlp_layer):
    x2 = x_ref[...] + _dot(o_ref[...], wo_ref[...])
    y_ref[...] = _mlp(x2, gm_ref[mlp_layer:mlp_layer + 1, :], wup_ref, wdn_ref)


def _out_mlp(x, o, w_o, g_mlp, mlp_layer, w_up, w_down):
    m, d = x.shape
    n = o.shape[1]
    d_ff = w_up.shape[2]
    tm = BIG_ROW_TILE
    assert m % tm == 0 and d_ff % FF_CHUNK == 0
    resident = pl.BlockSpec(memory_space=pltpu.VMEM)
    return pl.pallas_call(
        functools.partial(_out_mlp_kernel, mlp_layer=mlp_layer),
        grid=(m // tm,),
        in_specs=[pl.BlockSpec((tm, d), lambda i: (i, 0)),
                  pl.BlockSpec((tm, n), lambda i: (i, 0)),
                  resident, _const_spec(g_mlp.shape), resident, resident],
        out_specs=pl.BlockSpec((tm, d), lambda i: (i, 0)),
        out_shape=jax.ShapeDtypeStruct((m, d), _F32),
        compiler_params=pltpu.CompilerParams(
            dimension_semantics=("arbitrary",),
            vmem_limit_bytes=OUT_MLP_VMEM_LIMIT_BYTES),
        name="out_mlp",
    )(x, o, w_o.reshape(n, d), g_mlp, w_up.reshape(d, d_ff),
      w_down.reshape(d_ff, d))


def _lambda_init(layer_idx):
    return 0.8 - 0.6 * math.exp(-0.3 * layer_idx)


def kernel(x, a_norm, a_w_in, a_conv, a_w_out, kv_norm, w_kv, k_norm, b_norm,
           w_q, q_norm, lam_q1, lam_k1, lam_q2, lam_k2, sub_norm, w_o,
           mlp_norm, w_up, w_down):
    b, s, d = x.shape
    n_a, n_b = a_norm.shape[0], b_norm.shape[0]
    assert n_a == 1 and n_b == 1, "one mixer layer followed by one attention layer"
    n_qk = w_q.shape[2]
    assert w_kv.shape[1] == 2 * n_qk and n_qk % V7X_MXU_DIM == 0

    row = lambda g: g.reshape(1, -1)
    gid = np.arange(V7X_MXU_DIM) // HEAD_DIM
    ones = jnp.asarray(gid[:, None] == gid[None, :], dtype=_BF16)

    x, (w_up_bf0, w_down_bf0, w_kv_bf, w_q_bf) = _mixer(
        x, row(a_norm[0]), a_w_in, row(a_conv[0]), a_w_out, 0,
        [(w_up, 0), (w_down, 0), (w_kv[None], 0), (w_q, 0)])
    (x1, q, k, v), (w_o_bf, w_up_bf1, w_down_bf1) = _mlp_qkv(
        x.reshape(b * s, d), mlp_norm, 0, w_up_bf0, w_down_bf0,
        row(kv_norm), w_kv_bf, row(k_norm), row(b_norm[0]), w_q_bf,
        row(q_norm[0]), ones, [(w_o, 0), (w_up, 1), (w_down, 1)])
    o = _attention(q.reshape(b, s, n_qk), k.reshape(b, s, n_qk),
                   v.reshape(b, s, -1), row(lam_q1[0]), row(lam_k1[0]),
                   row(lam_q2[0]), row(lam_k2[0]), row(sub_norm[0]),
                   _lambda_init(n_a))
    y = _out_mlp(x1, o.reshape(b * s, -1), w_o_bf, mlp_norm, 1,
                 w_up_bf1, w_down_bf1)
    return y.reshape(b, s, d)
```

```python
import functools
import math

import jax
import jax.numpy as jnp
import numpy as np
from jax import lax
from jax.experimental import pallas as pl
from jax.experimental.pallas import tpu as pltpu

EPS = 1e-6
NEG_INF = -1e30
HEAD_DIM = 64
V_DIM = 2 * HEAD_DIM
CONV_WIDTH = 3

V7X_LANES = 128
V7X_SUBLANES = 8
V7X_MXU_DIM = 256
V7X_BF16_ROWS_PER_VREG = 16
V7X_VMEM_LIMIT_BYTES = 60 * 1024 * 1024
ATTN_VMEM_LIMIT_BYTES = 24 * 1024 * 1024
OUT_MLP_VMEM_LIMIT_BYTES = 34 * 1024 * 1024

ROW_TILE = 512
BIG_ROW_TILE = 1024
FF_CHUNK = 1024
Q_TILE = 256
K_TILE = 256
HEADS_PER_STEP = 2
CAST_ROWS = 128

_LOG2_E = math.log2(math.e)
_BF16 = jnp.bfloat16
_F32 = jnp.float32


def _dot(a, b):
    return jnp.dot(a, b, preferred_element_type=_F32)


def _dot_nt(a, b):
    return lax.dot_general(a, b, (((1,), (1,)), ((), ())),
                           preferred_element_type=_F32)


def _rms(x, g):
    ms = jnp.mean(x * x, axis=-1, keepdims=True)
    return x * lax.rsqrt(ms + EPS) * g


def _mlp(x, g, wup_ref, wdn_ref):
    h = _rms(x, g).astype(_BF16)
    acc = x
    d_ff = wup_ref.shape[1]
    for c in range(d_ff // FF_CHUNK):
        sl = slice(c * FF_CHUNK, (c + 1) * FF_CHUNK)
        a = jnp.maximum(_dot(h, wup_ref[:, sl]), 0.0)
        acc = acc + _dot((a * a).astype(_BF16), wdn_ref[sl, :])
    return acc


def _head_rms(y, ones_ref, gain):
    y2 = (y * y).astype(_BF16)
    n = y.shape[1]
    parts = [_dot(y2[:, t:t + V7X_MXU_DIM], ones_ref[...])
             for t in range(0, n, V7X_MXU_DIM)]
    ss = jnp.concatenate(parts, axis=1)
    gain_row = jnp.concatenate([gain] * (n // HEAD_DIM), axis=1)
    return y * lax.rsqrt(ss * (1.0 / HEAD_DIM) + EPS) * gain_row


def _const_spec(shape):
    return pl.BlockSpec(shape, lambda *_: (0,) * len(shape),
                        pipeline_mode=pl.Buffered(1))


def _layer_spec(w, layer):
    _, r, c = w.shape
    return pl.BlockSpec((None, r, c), lambda *_: (layer, 0, 0),
                        pipeline_mode=pl.Buffered(1))


def _cast_jobs(weights, n_steps, step_of):
    in_specs, out_specs, shapes = [], [], []
    for w, layer in weights:
        _, r, c = w.shape
        assert r % (n_steps * V7X_BF16_ROWS_PER_VREG) == 0
        block = (None, r // n_steps, c)
        in_specs.append(pl.BlockSpec(
            block, lambda *ids, layer=layer: (layer, step_of(*ids), 0)))
        out_specs.append(pl.BlockSpec(
            block, lambda *ids: (0, step_of(*ids), 0)))
        shapes.append(jax.ShapeDtypeStruct((1, r, c), _BF16))
    return in_specs, out_specs, shapes


def _run_casts(src_refs, dst_refs):
    for src, dst in zip(src_refs, dst_refs):
        dst[...] = src[...].astype(_BF16)


def _mixer_kernel(*refs, n_cast):
    x_ref, g_ref, win_ref, conv_ref, wout_ref = refs[:5]
    cast_src = refs[5:5 + n_cast]
    o_ref = refs[5 + n_cast]
    cast_dst = refs[6 + n_cast:6 + 2 * n_cast]
    carry_ref, win_bf, wout_bf = refs[6 + 2 * n_cast:]
    d = x_ref.shape[2]
    ts = x_ref.shape[1]

    @pl.when((pl.program_id(0) == 0) & (pl.program_id(1) == 0))
    def _():
        def cast_rows(i, _):
            r = pl.ds(pl.multiple_of(i * CAST_ROWS, CAST_ROWS), CAST_ROWS)
            win_bf[r, :] = win_ref[r, :].astype(_BF16)
            wout_bf[r, :] = wout_ref[r, :].astype(_BF16)
            return 0
        lax.fori_loop(0, d // CAST_ROWS, cast_rows, 0)

    @pl.when(pl.program_id(1) == 0)
    def _():
        carry_ref[...] = jnp.zeros_like(carry_ref)

    _run_casts(cast_src, cast_dst)

    x = x_ref[0]
    h = _rms(x, g_ref[...]).astype(_BF16)
    b_gate = _dot(h, win_bf[:, 0:d])
    c_gate = _dot(h, win_bf[:, d:2 * d])
    v = _dot(h, win_bf[:, 2 * d:3 * d])
    u = c_gate * v

    rows = lax.broadcasted_iota(jnp.int32, u.shape, 0)
    tail = carry_ref[...]
    prev1 = tail[V7X_SUBLANES - 1:V7X_SUBLANES]
    prev2 = tail[V7X_SUBLANES - 2:V7X_SUBLANES - 1]
    u1 = jnp.where(rows == 0, prev1, pltpu.roll(u, 1, 0))
    u2 = jnp.where(rows == 0, prev2,
                   jnp.where(rows == 1, prev1, pltpu.roll(u, 2, 0)))
    w = conv_ref[...]
    y = w[:, 0:d] * u2 + w[:, d:2 * d] * u1 + w[:, 2 * d:3 * d] * u
    carry_ref[...] = u[ts - V7X_SUBLANES:ts]

    o_ref[0] = x + _dot((b_gate * y).astype(_BF16), wout_bf[...])


def _mixer(x, g, w_in, conv_w, w_out, layer, next_weights):
    b, s, d = x.shape
    ts = BIG_ROW_TILE
    assert s % ts == 0 and d % CAST_ROWS == 0
    n_seq = s // ts
    tok = pl.BlockSpec((1, ts, d), lambda i, j: (i, j, 0))
    cast_in, cast_out, cast_shapes = _cast_jobs(next_weights, b * n_seq,
                                                lambda i, j: i * n_seq + j)
    outs = pl.pallas_call(
        functools.partial(_mixer_kernel, n_cast=len(cast_in)),
        grid=(b, n_seq),
        in_specs=[tok, _const_spec((1, d)), _layer_spec(w_in, layer),
                  _const_spec((1, CONV_WIDTH * d)), _layer_spec(w_out, layer),
                  *cast_in],
        out_specs=[tok, *cast_out],
        out_shape=[jax.ShapeDtypeStruct(x.shape, _F32), *cast_shapes],
        scratch_shapes=[pltpu.VMEM((V7X_SUBLANES, d), _F32),
                        pltpu.VMEM(w_in.shape[1:], _BF16),
                        pltpu.VMEM(w_out.shape[1:], _BF16)],
        compiler_params=pltpu.CompilerParams(
            dimension_semantics=("arbitrary", "arbitrary"),
            vmem_limit_bytes=V7X_VMEM_LIMIT_BYTES),
        name="mixer",
    )(x, g, w_in, conv_w, w_out, *[w for w, _ in next_weights])
    return outs[0], outs[1:]


def _mlp_qkv_kernel(*refs, n_cast, mlp_layer):
    (x_ref, gm_ref, wup_ref, wdn_ref, gkv_ref, wkt_ref, wkv_ref, gk_ref, gb_ref,
     wq_ref, gq_ref, ones_ref) = refs[:12]
    cast_src = refs[12:12 + n_cast]
    x1_ref, q_ref, kt_ref, v_ref = refs[12 + n_cast:16 + n_cast]
    cast_dst = refs[16 + n_cast:]

    _run_casts(cast_src, cast_dst)

    x1 = _mlp(x_ref[...], gm_ref[mlp_layer:mlp_layer + 1, :], wup_ref, wdn_ref)
    x1_ref[...] = x1

    nk = wkt_ref.shape[0]
    hkv = _rms(x1, gkv_ref[...]).astype(_BF16)
    kt = _dot_nt(wkt_ref[...], hkv)
    kt3 = kt.reshape(nk // HEAD_DIM, HEAD_DIM, kt.shape[1])
    ms = jnp.mean(kt3 * kt3, axis=1, keepdims=True)
    kt3 = kt3 * lax.rsqrt(ms + EPS) * gk_ref[...][None]
    kt_ref[0] = kt3.reshape(kt.shape).astype(_BF16)
    v_ref[...] = _dot(hkv, wkv_ref[:, nk:]).astype(_BF16)

    hq = _rms(x1, gb_ref[...]).astype(_BF16)
    q = _head_rms(_dot(hq, wq_ref[...]), ones_ref, gq_ref[...])
    q_ref[...] = (q * (HEAD_DIM ** -0.5 * _LOG2_E)).astype(_BF16)


def _mlp_qkv(x, seq_len, g_mlp, mlp_layer, w_up, w_down, g_kv, w_kt, w_kv, g_k,
             g_b, w_q, g_q, ones, next_weights):
    m, d = x.shape
    d_ff = w_up.shape[2]
    nq = w_q.shape[2]
    nkv = w_kv.shape[2]
    nk = w_kt.shape[1]
    tm = ROW_TILE
    assert seq_len % tm == 0 and m % seq_len == 0 and d_ff % FF_CHUNK == 0
    tiles_per_seq = seq_len // tm

    def rows(n):
        return pl.BlockSpec((tm, n), lambda i: (i, 0))

    cast_in, cast_out, cast_shapes = _cast_jobs(next_weights, m // tm,
                                                lambda i: i)
    outs = pl.pallas_call(
        functools.partial(_mlp_qkv_kernel, n_cast=len(cast_in),
                          mlp_layer=mlp_layer),
        grid=(m // tm,),
        in_specs=[rows(d), _const_spec(g_mlp.shape), _layer_spec(w_up, 0),
                  _layer_spec(w_down, 0), _const_spec((1, d)),
                  _layer_spec(w_kt, 0), _layer_spec(w_kv, 0),
                  _const_spec((HEAD_DIM, 1)),
                  _const_spec((1, d)), _layer_spec(w_q, 0),
                  _const_spec((1, HEAD_DIM)),
                  _const_spec((V7X_MXU_DIM, V7X_MXU_DIM)), *cast_in],
        out_specs=[rows(d), rows(nq),
                   pl.BlockSpec((1, nk, tm), lambda i: (i // tiles_per_seq, 0,
                                                        i % tiles_per_seq)),
                   rows(nkv - nk), *cast_out],
        out_shape=[jax.ShapeDtypeStruct((m, d), _F32),
                   jax.ShapeDtypeStruct((m, nq), _BF16),
                   jax.ShapeDtypeStruct((m // seq_len, nk, seq_len), _BF16),
                   jax.ShapeDtypeStruct((m, nkv - nk), _BF16), *cast_shapes],
        compiler_params=pltpu.CompilerParams(
            dimension_semantics=("arbitrary",),
            vmem_limit_bytes=V7X_VMEM_LIMIT_BYTES),
        name="mlp_qkv",
    )(x, g_mlp, w_up, w_down, g_kv, w_kt, w_kv, g_k, g_b, w_q, g_q, ones,
      *[w for w, _ in next_weights])
    return outs[:4], outs[4:]


def _attn_kernel(q_ref, kt_ref, v_ref, lq1_ref, lk1_ref, lq2_ref, lk2_ref,
                 sub_ref, o_ref, s_scr, p_scr, vx_scr, *, lam_init):
    s_len = q_ref.shape[1]
    n_heads = q_ref.shape[2] // V_DIM
    tq, tk = Q_TILE, K_TILE
    lam = (jnp.exp(jnp.sum(lq1_ref[...] * lk1_ref[...], keepdims=True))
           - jnp.exp(jnp.sum(lq2_ref[...] * lk2_ref[...], keepdims=True))
           + lam_init)
    out_gain = sub_ref[...] * (1.0 - lam_init)

    for h in range(n_heads):
        vx_scr[h, :, 0:V_DIM] = v_ref[0, :, h * V_DIM:(h + 1) * V_DIM]
        vx_scr[h, :, V_DIM:2 * V_DIM] = jnp.ones((s_len, V_DIM), _BF16)

    lane = lax.broadcasted_iota(jnp.int32, (tq, V_DIM), 1)
    row_id = lax.broadcasted_iota(jnp.int32, (tq, tk), 0)
    row_id = jnp.concatenate([row_id, row_id], axis=0)
    col_id = lax.broadcasted_iota(jnp.int32, (2 * tq, tk), 1)

    def lane_tiles(a):
        return [a[:, t:t + V7X_LANES] for t in range(0, a.shape[1], V7X_LANES)]

    tiles = [(h, qi) for h in range(n_heads)
             for qi in reversed(range(s_len // tq))]
    row_max = {}

    def scores(pos):
        (h, qi), slot = tiles[pos], pos % 2
        q_lo, cols = qi * tq, slice(h * V_DIM, (h + 1) * V_DIM)
        q = q_ref[0, q_lo:q_lo + tq, cols]
        zero = jnp.zeros_like(q)
        q2 = jnp.concatenate([jnp.where(lane < HEAD_DIM, q, zero),
                              jnp.where(lane >= HEAD_DIM, q, zero)], axis=0)
        m_run = jnp.full((2 * tq, V7X_LANES), NEG_INF, _F32)
        for k_lo in range(0, q_lo + tq, tk):
            s = _dot(q2, kt_ref[0, cols, k_lo:k_lo + tk])
            if k_lo + tk > q_lo:
                visible = (k_lo + col_id) <= (q_lo + row_id)
                s = jnp.where(visible, s, NEG_INF)
            s_scr[slot, :, k_lo:k_lo + tk] = s
            for part in lane_tiles(s):
                m_run = jnp.maximum(m_run, part)
            yield
        row_max[pos] = jnp.max(m_run, axis=-1, keepdims=True)

    def probs(pos):
        (_, qi), slot = tiles[pos], pos % 2
        for k_lo in range(0, qi * tq + tq, tk):
            p = jnp.exp2(s_scr[slot, :, k_lo:k_lo + tk] - row_max[pos])
            p_scr[slot, :, k_lo:k_lo + tk] = p.astype(_BF16)
            yield

    def values(pos):
        (h, qi), slot = tiles[pos], pos % 2
        q_lo, n_keys = qi * tq, qi * tq + tq
        pv = _dot(p_scr[slot, :, 0:n_keys], vx_scr[h, 0:n_keys, :])
        maps = pv[:, 0:V_DIM] / pv[:, V_DIM:2 * V_DIM]
        o = maps[0:tq] - lam * maps[tq:2 * tq]
        o_ref[0, q_lo:q_lo + tq, h * V_DIM:(h + 1) * V_DIM] = (
            _rms(o, out_gain).astype(_BF16))

    for _ in scores(0):
        pass
    for pos in range(len(tiles) + 1):
        if pos >= 1:
            values(pos - 1)
        active = [stage(p) for stage, p in ((probs, pos), (scores, pos + 1))
                  if p < len(tiles)]
        while active:
            for gen in list(active):
                if next(gen, StopIteration) is StopIteration:
                    active.remove(gen)


def _attention(q, kt, v, lq1, lk1, lq2, lk2, sub, lam_init):
    b, s, n = q.shape
    n_heads = n // V_DIM
    assert s % Q_TILE == 0 and s % K_TILE == 0 and n_heads % HEADS_PER_STEP == 0
    head = pl.BlockSpec((1, s, HEADS_PER_STEP * V_DIM), lambda i, h: (i, 0, h))
    head_t = pl.BlockSpec((1, HEADS_PER_STEP * V_DIM, s), lambda i, h: (i, h, 0))
    vec = _const_spec((1, HEAD_DIM))
    return pl.pallas_call(
        functools.partial(_attn_kernel, lam_init=lam_init),
        grid=(b, n_heads // HEADS_PER_STEP),
        in_specs=[head, head_t, head, vec, vec, vec, vec,
                  _const_spec((1, V_DIM))],
        out_specs=head,
        out_shape=jax.ShapeDtypeStruct((b, s, n), _BF16),
        scratch_shapes=[pltpu.VMEM((2, 2 * Q_TILE, s), _F32),
                        pltpu.VMEM((2, 2 * Q_TILE, s), _BF16),
                        pltpu.VMEM((HEADS_PER_STEP, s, 2 * V_DIM), _BF16)],
        compiler_params=pltpu.CompilerParams(
            dimension_semantics=("arbitrary", "arbitrary"),
            vmem_limit_bytes=ATTN_VMEM_LIMIT_BYTES),
        name="attention",
    )(q, kt, v, lq1, lk1, lq2, lk2, sub)


def _out_mlp_kernel(x_ref, o_ref, wo_ref, gm_ref, wup_ref, wdn_ref, y_ref, *,
                    mlp_layer):
    x2 = x_ref[...] + _dot(o_ref[...], wo_ref[...])
    y_ref[...] = _mlp(x2, gm_ref[mlp_layer:mlp_layer + 1, :], wup_ref, wdn_ref)


def _out_mlp(x, o, w_o, g_mlp, mlp_layer, w_up, w_down):
    m, d = x.shape
    n = o.shape[1]
    d_ff = w_up.shape[2]
    tm = BIG_ROW_TILE
    assert m % tm == 0 and d_ff % FF_CHUNK == 0
    resident = pl.BlockSpec(memory_space=pltpu.VMEM)
    return pl.pallas_call(
        functools.partial(_out_mlp_kernel, mlp_layer=mlp_layer),
        grid=(m // tm,),
        in_specs=[pl.BlockSpec((tm, d), lambda i: (i, 0)),
                  pl.BlockSpec((tm, n), lambda i: (i, 0)),
                  resident, _const_spec(g_mlp.shape), resident, resident],
        out_specs=pl.BlockSpec((tm, d), lambda i: (i, 0)),
        out_shape=jax.ShapeDtypeStruct((m, d), _F32),
        compiler_params=pltpu.CompilerParams(
            dimension_semantics=("arbitrary",),
            vmem_limit_bytes=OUT_MLP_VMEM_LIMIT_BYTES),
        name="out_mlp",
    )(x, o, w_o.reshape(n, d), g_mlp, w_up.reshape(d, d_ff),
      w_down.reshape(d_ff, d))


def _lambda_init(layer_idx):
    return 0.8 - 0.6 * math.exp(-0.3 * layer_idx)


def kernel(x, a_norm, a_w_in, a_conv, a_w_out, kv_norm, w_kv, k_norm, b_norm,
           w_q, q_norm, lam_q1, lam_k1, lam_q2, lam_k2, sub_norm, w_o,
           mlp_norm, w_up, w_down):
    b, s, d = x.shape
    n_a, n_b = a_norm.shape[0], b_norm.shape[0]
    assert n_a == 1 and n_b == 1, "one mixer layer followed by one attention layer"
    n_qk = w_q.shape[2]
    assert w_kv.shape[1] == 2 * n_qk and n_qk % V7X_MXU_DIM == 0

    row = lambda g: g.reshape(1, -1)
    gid = np.arange(V7X_MXU_DIM) // HEAD_DIM
    ones = jnp.asarray(gid[:, None] == gid[None, :], dtype=_BF16)

    w_kt = w_kv[:, :n_qk].T[None]
    x, (w_up_bf0, w_down_bf0, w_kt_bf, w_kv_bf, w_q_bf) = _mixer(
        x, row(a_norm[0]), a_w_in, row(a_conv[0]), a_w_out, 0,
        [(w_up, 0), (w_down, 0), (w_kt, 0), (w_kv[None], 0), (w_q, 0)])
    (x1, q, kt, v), (w_o_bf, w_up_bf1, w_down_bf1) = _mlp_qkv(
        x.reshape(b * s, d), s, mlp_norm, 0, w_up_bf0, w_down_bf0,
        row(kv_norm), w_kt_bf, w_kv_bf, k_norm.reshape(-1, 1), row(b_norm[0]),
        w_q_bf, row(q_norm[0]), ones, [(w_o, 0), (w_up, 1), (w_down, 1)])
    o = _attention(q.reshape(b, s, n_qk), kt,
                   v.reshape(b, s, -1), row(lam_q1[0]), row(lam_k1[0]),
                   row(lam_q2[0]), row(lam_k2[0]), row(sub_norm[0]),
                   _lambda_init(n_a))
    y = _out_mlp(x1, o.reshape(b * s, -1), w_o_bf, mlp_norm, 1,
                 w_up_bf1, w_down_bf1)
    return y.reshape(b, s, d)
```

```python
import functools
import math

import jax
import jax.numpy as jnp
import numpy as np
from jax import lax
from jax.experimental import pallas as pl
from jax.experimental.pallas import tpu as pltpu

EPS = 1e-6
NEG_INF = -1e30
HEAD_DIM = 64
V_DIM = 2 * HEAD_DIM
CONV_WIDTH = 3

V7X_LANES = 128
V7X_SUBLANES = 8
V7X_MXU_DIM = 256
V7X_BF16_ROWS_PER_VREG = 16
V7X_VMEM_LIMIT_BYTES = 60 * 1024 * 1024
ATTN_VMEM_LIMIT_BYTES = 24 * 1024 * 1024
OUT_MLP_VMEM_LIMIT_BYTES = 34 * 1024 * 1024
MLP_VMEM_LIMIT_BYTES = 52 * 1024 * 1024
QKV_VMEM_LIMIT_BYTES = 40 * 1024 * 1024

BIG_ROW_TILE = 1024
FF_CHUNK = 1024
Q_TILE = 256
K_TILE = 256
HEADS_PER_STEP = 2
CAST_ROWS = 128

_LOG2_E = math.log2(math.e)
_BF16 = jnp.bfloat16
_F32 = jnp.float32


def _dot(a, b):
    return jnp.dot(a, b, preferred_element_type=_F32)


def _dot_nt(a, b):
    return lax.dot_general(a, b, (((1,), (1,)), ((), ())),
                           preferred_element_type=_F32)


def _rms(x, g):
    ms = jnp.mean(x * x, axis=-1, keepdims=True)
    return x * lax.rsqrt(ms + EPS) * g


def _mlp(x, g, wup_ref, wdn_ref):
    h = _rms(x, g).astype(_BF16)
    acc = x
    d_ff = wup_ref.shape[1]
    for c in range(d_ff // FF_CHUNK):
        sl = slice(c * FF_CHUNK, (c + 1) * FF_CHUNK)
        a = jnp.maximum(_dot(h, wup_ref[:, sl]), 0.0)
        acc = acc + _dot((a * a).astype(_BF16), wdn_ref[sl, :])
    return acc


def _head_rms(y, ones_ref, gain):
    y2 = (y * y).astype(_BF16)
    n = y.shape[1]
    parts = [_dot(y2[:, t:t + V7X_MXU_DIM], ones_ref[...])
             for t in range(0, n, V7X_MXU_DIM)]
    ss = jnp.concatenate(parts, axis=1)
    gain_row = jnp.concatenate([gain] * (n // HEAD_DIM), axis=1)
    return y * lax.rsqrt(ss * (1.0 / HEAD_DIM) + EPS) * gain_row


def _const_spec(shape):
    return pl.BlockSpec(shape, lambda *_: (0,) * len(shape),
                        pipeline_mode=pl.Buffered(1))


def _layer_spec(w, layer):
    _, r, c = w.shape
    return pl.BlockSpec((None, r, c), lambda *_: (layer, 0, 0),
                        pipeline_mode=pl.Buffered(1))


def _cast_jobs(weights, n_steps, step_of):
    in_specs, out_specs, shapes = [], [], []
    for w, layer in weights:
        _, r, c = w.shape
        assert r % (n_steps * V7X_BF16_ROWS_PER_VREG) == 0
        block = (None, r // n_steps, c)
        in_specs.append(pl.BlockSpec(
            block, lambda *ids, layer=layer: (layer, step_of(*ids), 0)))
        out_specs.append(pl.BlockSpec(
            block, lambda *ids: (0, step_of(*ids), 0)))
        shapes.append(jax.ShapeDtypeStruct((1, r, c), _BF16))
    return in_specs, out_specs, shapes


def _run_casts(src_refs, dst_refs):
    for src, dst in zip(src_refs, dst_refs):
        dst[...] = src[...].astype(_BF16)


def _mixer_kernel(*refs, n_cast):
    x_ref, g_ref, win_ref, conv_ref, wout_ref = refs[:5]
    cast_src = refs[5:5 + n_cast]
    o_ref = refs[5 + n_cast]
    cast_dst = refs[6 + n_cast:6 + 2 * n_cast]
    carry_ref, win_bf, wout_bf = refs[6 + 2 * n_cast:]
    d = x_ref.shape[2]
    ts = x_ref.shape[1]

    @pl.when((pl.program_id(0) == 0) & (pl.program_id(1) == 0))
    def _():
        def cast_rows(i, _):
            r = pl.ds(pl.multiple_of(i * CAST_ROWS, CAST_ROWS), CAST_ROWS)
            win_bf[r, :] = win_ref[r, :].astype(_BF16)
            wout_bf[r, :] = wout_ref[r, :].astype(_BF16)
            return 0
        lax.fori_loop(0, d // CAST_ROWS, cast_rows, 0)

    @pl.when(pl.program_id(1) == 0)
    def _():
        carry_ref[...] = jnp.zeros_like(carry_ref)

    _run_casts(cast_src, cast_dst)

    x = x_ref[0]
    h = _rms(x, g_ref[...]).astype(_BF16)
    b_gate = _dot(h, win_bf[:, 0:d])
    c_gate = _dot(h, win_bf[:, d:2 * d])
    v = _dot(h, win_bf[:, 2 * d:3 * d])
    u = c_gate * v

    rows = lax.broadcasted_iota(jnp.int32, u.shape, 0)
    tail = carry_ref[...]
    prev1 = tail[V7X_SUBLANES - 1:V7X_SUBLANES]
    prev2 = tail[V7X_SUBLANES - 2:V7X_SUBLANES - 1]
    u1 = jnp.where(rows == 0, prev1, pltpu.roll(u, 1, 0))
    u2 = jnp.where(rows == 0, prev2,
                   jnp.where(rows == 1, prev1, pltpu.roll(u, 2, 0)))
    w = conv_ref[...]
    y = w[:, 0:d] * u2 + w[:, d:2 * d] * u1 + w[:, 2 * d:3 * d] * u
    carry_ref[...] = u[ts - V7X_SUBLANES:ts]

    o_ref[0] = x + _dot((b_gate * y).astype(_BF16), wout_bf[...])


def _mixer(x, g, w_in, conv_w, w_out, layer, next_weights):
    b, s, d = x.shape
    ts = BIG_ROW_TILE
    assert s % ts == 0 and d % CAST_ROWS == 0
    n_seq = s // ts
    tok = pl.BlockSpec((1, ts, d), lambda i, j: (i, j, 0))
    cast_in, cast_out, cast_shapes = _cast_jobs(next_weights, b * n_seq,
                                                lambda i, j: i * n_seq + j)
    outs = pl.pallas_call(
        functools.partial(_mixer_kernel, n_cast=len(cast_in)),
        grid=(b, n_seq),
        in_specs=[tok, _const_spec((1, d)), _layer_spec(w_in, layer),
                  _const_spec((1, CONV_WIDTH * d)), _layer_spec(w_out, layer),
                  *cast_in],
        out_specs=[tok, *cast_out],
        out_shape=[jax.ShapeDtypeStruct(x.shape, _F32), *cast_shapes],
        scratch_shapes=[pltpu.VMEM((V7X_SUBLANES, d), _F32),
                        pltpu.VMEM(w_in.shape[1:], _BF16),
                        pltpu.VMEM(w_out.shape[1:], _BF16)],
        compiler_params=pltpu.CompilerParams(
            dimension_semantics=("arbitrary", "arbitrary"),
            vmem_limit_bytes=V7X_VMEM_LIMIT_BYTES),
        name="mixer",
    )(x, g, w_in, conv_w, w_out, *[w for w, _ in next_weights])
    return outs[0], outs[1:]


def _mlp_kernel(*refs, n_cast, mlp_layer):
    x_ref, gm_ref, wup_ref, wdn_ref = refs[:4]
    cast_src = refs[4:4 + n_cast]
    x1_ref = refs[4 + n_cast]
    cast_dst = refs[5 + n_cast:]
    _run_casts(cast_src, cast_dst)
    x1_ref[...] = _mlp(x_ref[...], gm_ref[mlp_layer:mlp_layer + 1, :], wup_ref,
                       wdn_ref)


def _mlp_call(x, g_mlp, mlp_layer, w_up, w_down, next_weights):
    m, d = x.shape
    d_ff = w_up.shape[2]
    tm = BIG_ROW_TILE
    assert m % tm == 0 and d_ff % FF_CHUNK == 0
    rows = pl.BlockSpec((tm, d), lambda i: (i, 0))
    cast_in, cast_out, cast_shapes = _cast_jobs(next_weights, m // tm,
                                                lambda i: i)
    outs = pl.pallas_call(
        functools.partial(_mlp_kernel, n_cast=len(cast_in), mlp_layer=mlp_layer),
        grid=(m // tm,),
        in_specs=[rows, _const_spec(g_mlp.shape), _layer_spec(w_up, 0),
                  _layer_spec(w_down, 0), *cast_in],
        out_specs=[rows, *cast_out],
        out_shape=[jax.ShapeDtypeStruct((m, d), _F32), *cast_shapes],
        compiler_params=pltpu.CompilerParams(
            dimension_semantics=("arbitrary",),
            vmem_limit_bytes=MLP_VMEM_LIMIT_BYTES),
        name="mlp",
    )(x, g_mlp, w_up, w_down, *[w for w, _ in next_weights])
    return outs[0], outs[1:]


def _qkv_kernel(x1_ref, gkv_ref, wkv_ref, gk_ref, gb_ref, wq_ref, gq_ref,
                ones_ref, q_ref, k_ref, v_ref):
    x1 = x1_ref[...]
    nk = k_ref.shape[1]
    hkv = _rms(x1, gkv_ref[...]).astype(_BF16)
    k = _dot(hkv, wkv_ref[:, 0:nk])
    k_ref[...] = _head_rms(k, ones_ref, gk_ref[...]).astype(_BF16)
    v_ref[...] = _dot(hkv, wkv_ref[:, nk:]).astype(_BF16)

    hq = _rms(x1, gb_ref[...]).astype(_BF16)
    q = _head_rms(_dot(hq, wq_ref[...]), ones_ref, gq_ref[...])
    q_ref[...] = (q * (HEAD_DIM ** -0.5 * _LOG2_E)).astype(_BF16)


def _qkv_call(x1, g_kv, w_kv, g_k, g_b, w_q, g_q, ones):
    m, d = x1.shape
    nq = w_q.shape[1]
    nkv = w_kv.shape[1]
    nk = nq
    tm = BIG_ROW_TILE
    assert m % tm == 0

    def rows(n):
        return pl.BlockSpec((tm, n), lambda i: (i, 0))

    resident = pl.BlockSpec(memory_space=pltpu.VMEM)
    return pl.pallas_call(
        _qkv_kernel,
        grid=(m // tm,),
        in_specs=[rows(d), _const_spec((1, d)), resident,
                  _const_spec((1, HEAD_DIM)), _const_spec((1, d)), resident,
                  _const_spec((1, HEAD_DIM)),
                  _const_spec((V7X_MXU_DIM, V7X_MXU_DIM))],
        out_specs=[rows(nq), rows(nk), rows(nkv - nk)],
        out_shape=[jax.ShapeDtypeStruct((m, nq), _BF16),
                   jax.ShapeDtypeStruct((m, nk), _BF16),
                   jax.ShapeDtypeStruct((m, nkv - nk), _BF16)],
        compiler_params=pltpu.CompilerParams(
            dimension_semantics=("arbitrary",),
            vmem_limit_bytes=QKV_VMEM_LIMIT_BYTES),
        name="qkv",
    )(x1, g_kv, w_kv, g_k, g_b, w_q, g_q, ones)


def _attn_kernel(q_ref, k_ref, v_ref, lq1_ref, lk1_ref, lq2_ref, lk2_ref,
                 sub_ref, o_ref, s_scr, p_scr, vx_scr, *, lam_init):
    s_len = q_ref.shape[1]
    n_heads = q_ref.shape[2] // V_DIM
    tq, tk = Q_TILE, K_TILE
    lam = (jnp.exp(jnp.sum(lq1_ref[...] * lk1_ref[...], keepdims=True))
           - jnp.exp(jnp.sum(lq2_ref[...] * lk2_ref[...], keepdims=True))
           + lam_init)
    out_gain = sub_ref[...] * (1.0 - lam_init)

    for h in range(n_heads):
        vx_scr[h, :, 0:V_DIM] = v_ref[0, :, h * V_DIM:(h + 1) * V_DIM]
        vx_scr[h, :, V_DIM:2 * V_DIM] = jnp.ones((s_len, V_DIM), _BF16)

    lane = lax.broadcasted_iota(jnp.int32, (tq, V_DIM), 1)
    row_id = lax.broadcasted_iota(jnp.int32, (tq, tk), 0)
    row_id = jnp.concatenate([row_id, row_id], axis=0)
    col_id = lax.broadcasted_iota(jnp.int32, (2 * tq, tk), 1)

    def lane_tiles(a):
        return [a[:, t:t + V7X_LANES] for t in range(0, a.shape[1], V7X_LANES)]

    tiles = [(h, qi) for h in range(n_heads)
             for qi in reversed(range(s_len // tq))]
    row_max = {}

    def scores(pos):
        (h, qi), slot = tiles[pos], pos % 2
        q_lo, cols = qi * tq, slice(h * V_DIM, (h + 1) * V_DIM)
        q = q_ref[0, q_lo:q_lo + tq, cols]
        zero = jnp.zeros_like(q)
        q2 = jnp.concatenate([jnp.where(lane < HEAD_DIM, q, zero),
                              jnp.where(lane >= HEAD_DIM, q, zero)], axis=0)
        m_run = jnp.full((2 * tq, V7X_LANES), NEG_INF, _F32)
        for k_lo in range(0, q_lo + tq, tk):
            s = _dot_nt(q2, k_ref[0, k_lo:k_lo + tk, cols])
            if k_lo + tk > q_lo:
                visible = (k_lo + col_id) <= (q_lo + row_id)
                s = jnp.where(visible, s, NEG_INF)
            s_scr[slot, :, k_lo:k_lo + tk] = s
            for part in lane_tiles(s):
                m_run = jnp.maximum(m_run, part)
            yield
        row_max[pos] = jnp.max(m_run, axis=-1, keepdims=True)

    def probs(pos):
        (_, qi), slot = tiles[pos], pos % 2
        for k_lo in range(0, qi * tq + tq, tk):
            p = jnp.exp2(s_scr[slot, :, k_lo:k_lo + tk] - row_max[pos])
            p_scr[slot, :, k_lo:k_lo + tk] = p.astype(_BF16)
            yield

    def values(pos):
        (h, qi), slot = tiles[pos], pos % 2
        q_lo, n_keys = qi * tq, qi * tq + tq
        pv = _dot(p_scr[slot, :, 0:n_keys], vx_scr[h, 0:n_keys, :])
        maps = pv[:, 0:V_DIM] / pv[:, V_DIM:2 * V_DIM]
        o = maps[0:tq] - lam * maps[tq:2 * tq]
        o_ref[0, q_lo:q_lo + tq, h * V_DIM:(h + 1) * V_DIM] = (
            _rms(o, out_gain).astype(_BF16))

    for _ in scores(0):
        pass
    for pos in range(len(tiles) + 1):
        if pos >= 1:
            values(pos - 1)
        active = [stage(p) for stage, p in ((probs, pos), (scores, pos + 1))
                  if p < len(tiles)]
        while active:
            for gen in list(active):
                if next(gen, StopIteration) is StopIteration:
                    active.remove(gen)


def _attention(q, k, v, lq1, lk1, lq2, lk2, sub, lam_init):
    b, s, n = q.shape
    n_heads = n // V_DIM
    assert s % Q_TILE == 0 and s % K_TILE == 0 and n_heads % HEADS_PER_STEP == 0
    head = pl.BlockSpec((1, s, HEADS_PER_STEP * V_DIM), lambda i, h: (i, 0, h))
    vec = _const_spec((1, HEAD_DIM))
    return pl.pallas_call(
        functools.partial(_attn_kernel, lam_init=lam_init),
        grid=(b, n_heads // HEADS_PER_STEP),
        in_specs=[head, head, head, vec, vec, vec, vec, _const_spec((1, V_DIM))],
        out_specs=head,
        out_shape=jax.ShapeDtypeStruct((b, s, n), _BF16),
        scratch_shapes=[pltpu.VMEM((2, 2 * Q_TILE, s), _F32),
                        pltpu.VMEM((2, 2 * Q_TILE, s), _BF16),
                        pltpu.VMEM((HEADS_PER_STEP, s, 2 * V_DIM), _BF16)],
        compiler_params=pltpu.CompilerParams(
            dimension_semantics=("arbitrary", "arbitrary"),
            vmem_limit_bytes=ATTN_VMEM_LIMIT_BYTES),
        name="attention",
    )(q, k, v, lq1, lk1, lq2, lk2, sub)


def _out_mlp_kernel(x_ref, o_ref, wo_ref, gm_ref, wup_ref, wdn_ref, y_ref, *,
                    mlp_layer):
    x2 = x_ref[...] + _dot(o_ref[...], wo_ref[...])
    y_ref[...] = _mlp(x2, gm_ref[mlp_layer:mlp_layer + 1, :], wup_ref, wdn_ref)


def _out_mlp(x, o, w_o, g_mlp, mlp_layer, w_up, w_down):
    m, d = x.shape
    n = o.shape[1]
    d_ff = w_up.shape[2]
    tm = BIG_ROW_TILE
    assert m % tm == 0 and d_ff % FF_CHUNK == 0
    resident = pl.BlockSpec(memory_space=pltpu.VMEM)
    return pl.pallas_call(
        functools.partial(_out_mlp_kernel, mlp_layer=mlp_layer),
        grid=(m // tm,),
        in_specs=[pl.BlockSpec((tm, d), lambda i: (i, 0)),
                  pl.BlockSpec((tm, n), lambda i: (i, 0)),
                  resident, _const_spec(g_mlp.shape), resident, resident],
        out_specs=pl.BlockSpec((tm, d), lambda i: (i, 0)),
        out_shape=jax.ShapeDtypeStruct((m, d), _F32),
        compiler_params=pltpu.CompilerParams(
            dimension_semantics=("arbitrary",),
            vmem_limit_bytes=OUT_MLP_VMEM_LIMIT_BYTES),
        name="out_mlp",
    )(x, o, w_o.reshape(n, d), g_mlp, w_up.reshape(d, d_ff),
      w_down.reshape(d_ff, d))


def _lambda_init(layer_idx):
    return 0.8 - 0.6 * math.exp(-0.3 * layer_idx)


def kernel(x, a_norm, a_w_in, a_conv, a_w_out, kv_norm, w_kv, k_norm, b_norm,
           w_q, q_norm, lam_q1, lam_k1, lam_q2, lam_k2, sub_norm, w_o,
           mlp_norm, w_up, w_down):
    b, s, d = x.shape
    n_a, n_b = a_norm.shape[0], b_norm.shape[0]
    assert n_a == 1 and n_b == 1, "one mixer layer followed by one attention layer"
    n_qk = w_q.shape[2]
    assert w_kv.shape[1] == 2 * n_qk and n_qk % V7X_MXU_DIM == 0

    row = lambda g: g.reshape(1, -1)
    gid = np.arange(V7X_MXU_DIM) // HEAD_DIM
    ones = jnp.asarray(gid[:, None] == gid[None, :], dtype=_BF16)

    x, (w_up_bf0, w_down_bf0, w_kv_bf, w_q_bf) = _mixer(
        x, row(a_norm[0]), a_w_in, row(a_conv[0]), a_w_out, 0,
        [(w_up, 0), (w_down, 0), (w_kv[None], 0), (w_q, 0)])
    x1, (w_o_bf, w_up_bf1, w_down_bf1) = _mlp_call(
        x.reshape(b * s, d), mlp_norm, 0, w_up_bf0, w_down_bf0,
        [(w_o, 0), (w_up, 1), (w_down, 1)])
    q, k, v = _qkv_call(x1, row(kv_norm), w_kv_bf.reshape(d, -1), row(k_norm),
                        row(b_norm[0]), w_q_bf.reshape(d, -1), row(q_norm[0]),
                        ones)
    o = _attention(q.reshape(b, s, n_qk), k.reshape(b, s, n_qk),
                   v.reshape(b, s, -1), row(lam_q1[0]), row(lam_k1[0]),
                   row(lam_q2[0]), row(lam_k2[0]), row(sub_norm[0]),
                   _lambda_init(n_a))
    y = _out_mlp(x1, o.reshape(b * s, -1), w_o_bf, mlp_norm, 1,
                 w_up_bf1, w_down_bf1)
    return y.reshape(b, s, d)
```

```python
import functools
import math

import jax
import jax.numpy as jnp
import numpy as np
from jax import lax
from jax.experimental import pallas as pl
from jax.experimental.pallas import tpu as pltpu

EPS = 1e-6
NEG_INF = -1e30
HEAD_DIM = 64
V_DIM = 2 * HEAD_DIM
CONV_WIDTH = 3

V7X_LANES = 128
V7X_SUBLANES = 8
V7X_MXU_DIM = 256
V7X_BF16_ROWS_PER_VREG = 16
V7X_VMEM_LIMIT_BYTES = 60 * 1024 * 1024
ATTN_VMEM_LIMIT_BYTES = 24 * 1024 * 1024
OUT_MLP_VMEM_LIMIT_BYTES = 34 * 1024 * 1024

ROW_TILE = 512
BIG_ROW_TILE = 1024
FF_CHUNK = 1024
Q_TILE = 256
K_TILE = 256
HEADS_PER_STEP = 2
CAST_ROWS = 128

_LOG2_E = math.log2(math.e)
_BF16 = jnp.bfloat16
_F32 = jnp.float32


def _dot(a, b):
    return jnp.dot(a, b, preferred_element_type=_F32)


def _dot_nt(a, b):
    return lax.dot_general(a, b, (((1,), (1,)), ((), ())),
                           preferred_element_type=_F32)


def _rms(x, g):
    ms = jnp.mean(x * x, axis=-1, keepdims=True)
    return x * lax.rsqrt(ms + EPS) * g


def _mlp(x, g, wup_ref, wdn_ref):
    h = _rms(x, g).astype(_BF16)
    acc = x
    d_ff = wup_ref.shape[1]
    for c in range(d_ff // FF_CHUNK):
        sl = slice(c * FF_CHUNK, (c + 1) * FF_CHUNK)
        a = jnp.maximum(_dot(h, wup_ref[:, sl]), 0.0)
        acc = acc + _dot((a * a).astype(_BF16), wdn_ref[sl, :])
    return acc


def _head_rms(y, ones_ref, gain):
    y2 = (y * y).astype(_BF16)
    n = y.shape[1]
    parts = [_dot(y2[:, t:t + V7X_MXU_DIM], ones_ref[...])
             for t in range(0, n, V7X_MXU_DIM)]
    ss = jnp.concatenate(parts, axis=1)
    gain_row = jnp.concatenate([gain] * (n // HEAD_DIM), axis=1)
    return y * lax.rsqrt(ss * (1.0 / HEAD_DIM) + EPS) * gain_row


def _const_spec(shape):
    return pl.BlockSpec(shape, lambda *_: (0,) * len(shape),
                        pipeline_mode=pl.Buffered(1))


def _layer_spec(w, layer):
    _, r, c = w.shape
    return pl.BlockSpec((None, r, c), lambda *_: (layer, 0, 0),
                        pipeline_mode=pl.Buffered(1))


def _cast_jobs(weights, n_steps, step_of):
    in_specs, out_specs, shapes = [], [], []
    for w, layer in weights:
        _, r, c = w.shape
        assert r % (n_steps * V7X_BF16_ROWS_PER_VREG) == 0
        block = (None, r // n_steps, c)
        in_specs.append(pl.BlockSpec(
            block, lambda *ids, layer=layer: (layer, step_of(*ids), 0)))
        out_specs.append(pl.BlockSpec(
            block, lambda *ids: (0, step_of(*ids), 0)))
        shapes.append(jax.ShapeDtypeStruct((1, r, c), _BF16))
    return in_specs, out_specs, shapes


def _run_casts(src_refs, dst_refs):
    for src, dst in zip(src_refs, dst_refs):
        dst[...] = src[...].astype(_BF16)


def _mixer_kernel(*refs, n_cast):
    x_ref, g_ref, win_ref, conv_ref, wout_ref = refs[:5]
    cast_src = refs[5:5 + n_cast]
    o_ref = refs[5 + n_cast]
    cast_dst = refs[6 + n_cast:6 + 2 * n_cast]
    carry_ref, win_bf, wout_bf = refs[6 + 2 * n_cast:]
    d = x_ref.shape[2]
    ts = x_ref.shape[1]

    @pl.when((pl.program_id(0) == 0) & (pl.program_id(1) == 0))
    def _():
        def cast_rows(i, _):
            r = pl.ds(pl.multiple_of(i * CAST_ROWS, CAST_ROWS), CAST_ROWS)
            win_bf[r, :] = win_ref[r, :].astype(_BF16)
            wout_bf[r, :] = wout_ref[r, :].astype(_BF16)
            return 0
        lax.fori_loop(0, d // CAST_ROWS, cast_rows, 0)

    @pl.when(pl.program_id(1) == 0)
    def _():
        carry_ref[...] = jnp.zeros_like(carry_ref)

    _run_casts(cast_src, cast_dst)

    x = x_ref[0]
    h = _rms(x, g_ref[...]).astype(_BF16)
    b_gate = _dot(h, win_bf[:, 0:d])
    c_gate = _dot(h, win_bf[:, d:2 * d])
    v = _dot(h, win_bf[:, 2 * d:3 * d])
    u = c_gate * v

    rows = lax.broadcasted_iota(jnp.int32, u.shape, 0)
    tail = carry_ref[...]
    prev1 = tail[V7X_SUBLANES - 1:V7X_SUBLANES]
    prev2 = tail[V7X_SUBLANES - 2:V7X_SUBLANES - 1]
    u1 = jnp.where(rows == 0, prev1, pltpu.roll(u, 1, 0))
    u2 = jnp.where(rows == 0, prev2,
                   jnp.where(rows == 1, prev1, pltpu.roll(u, 2, 0)))
    w = conv_ref[...]
    y = w[:, 0:d] * u2 + w[:, d:2 * d] * u1 + w[:, 2 * d:3 * d] * u
    carry_ref[...] = u[ts - V7X_SUBLANES:ts]

    o_ref[0] = x + _dot((b_gate * y).astype(_BF16), wout_bf[...])


def _mixer(x, g, w_in, conv_w, w_out, layer, next_weights):
    b, s, d = x.shape
    ts = BIG_ROW_TILE
    assert s % ts == 0 and d % CAST_ROWS == 0
    n_seq = s // ts
    tok = pl.BlockSpec((1, ts, d), lambda i, j: (i, j, 0))
    cast_in, cast_out, cast_shapes = _cast_jobs(next_weights, b * n_seq,
                                                lambda i, j: i * n_seq + j)
    outs = pl.pallas_call(
        functools.partial(_mixer_kernel, n_cast=len(cast_in)),
        grid=(b, n_seq),
        in_specs=[tok, _const_spec((1, d)), _layer_spec(w_in, layer),
                  _const_spec((1, CONV_WIDTH * d)), _layer_spec(w_out, layer),
                  *cast_in],
        out_specs=[tok, *cast_out],
        out_shape=[jax.ShapeDtypeStruct(x.shape, _F32), *cast_shapes],
        scratch_shapes=[pltpu.VMEM((V7X_SUBLANES, d), _F32),
                        pltpu.VMEM(w_in.shape[1:], _BF16),
                        pltpu.VMEM(w_out.shape[1:], _BF16)],
        compiler_params=pltpu.CompilerParams(
            dimension_semantics=("arbitrary", "arbitrary"),
            vmem_limit_bytes=V7X_VMEM_LIMIT_BYTES),
        name="mixer",
    )(x, g, w_in, conv_w, w_out, *[w for w, _ in next_weights])
    return outs[0], outs[1:]


def _mlp_qkv_kernel(*refs, n_cast, mlp_layer):
    (x_ref, gm_ref, wup_ref, wdn_ref, gkv_ref, wkv_ref, gk_ref, gb_ref, wq_ref,
     gq_ref, ones_ref) = refs[:11]
    cast_src = refs[11:11 + n_cast]
    x1_ref, q_ref, k_ref, v_ref = refs[11 + n_cast:15 + n_cast]
    cast_dst = refs[15 + n_cast:]

    _run_casts(cast_src, cast_dst)

    x1 = _mlp(x_ref[...], gm_ref[mlp_layer:mlp_layer + 1, :], wup_ref, wdn_ref)
    x1_ref[...] = x1

    nk = k_ref.shape[1]
    hkv = _rms(x1, gkv_ref[...]).astype(_BF16)
    k = _dot(hkv, wkv_ref[:, 0:nk])
    k_ref[...] = _head_rms(k, ones_ref, gk_ref[...]).astype(_BF16)
    v_ref[...] = _dot(hkv, wkv_ref[:, nk:]).astype(_BF16)

    hq = _rms(x1, gb_ref[...]).astype(_BF16)
    q = _head_rms(_dot(hq, wq_ref[...]), ones_ref, gq_ref[...])
    q_ref[...] = (q * (HEAD_DIM ** -0.5 * _LOG2_E)).astype(_BF16)


def _mlp_qkv(x, g_mlp, mlp_layer, w_up, w_down, g_kv, w_kv, g_k, g_b, w_q, g_q,
             ones, next_weights):
    m, d = x.shape
    d_ff = w_up.shape[2]
    nq = w_q.shape[2]
    nkv = w_kv.shape[2]
    nk = nq
    tm = ROW_TILE
    assert m % tm == 0 and d_ff % FF_CHUNK == 0

    def rows(n):
        return pl.BlockSpec((tm, n), lambda i: (i, 0))

    cast_in, cast_out, cast_shapes = _cast_jobs(next_weights, m // tm,
                                                lambda i: i)
    outs = pl.pallas_call(
        functools.partial(_mlp_qkv_kernel, n_cast=len(cast_in),
                          mlp_layer=mlp_layer),
        grid=(m // tm,),
        in_specs=[rows(d), _const_spec(g_mlp.shape), _layer_spec(w_up, 0),
                  _layer_spec(w_down, 0), _const_spec((1, d)),
                  _layer_spec(w_kv, 0), _const_spec((1, HEAD_DIM)),
                  _const_spec((1, d)), _layer_spec(w_q, 0),
                  _const_spec((1, HEAD_DIM)),
                  _const_spec((V7X_MXU_DIM, V7X_MXU_DIM)), *cast_in],
        out_specs=[rows(d), rows(nq), rows(nk), rows(nkv - nk), *cast_out],
        out_shape=[jax.ShapeDtypeStruct((m, d), _F32),
                   jax.ShapeDtypeStruct((m, nq), _BF16),
                   jax.ShapeDtypeStruct((m, nk), _BF16),
                   jax.ShapeDtypeStruct((m, nkv - nk), _BF16), *cast_shapes],
        compiler_params=pltpu.CompilerParams(
            dimension_semantics=("arbitrary",),
            vmem_limit_bytes=V7X_VMEM_LIMIT_BYTES),
        name="mlp_qkv",
    )(x, g_mlp, w_up, w_down, g_kv, w_kv, g_k, g_b, w_q, g_q, ones,
      *[w for w, _ in next_weights])
    return outs[:4], outs[4:]


def _attn_kernel(q_ref, k_ref, v_ref, lq1_ref, lk1_ref, lq2_ref, lk2_ref,
                 sub_ref, o_ref, s_scr, p_scr, vx_scr, *, lam_init):
    s_len = q_ref.shape[1]
    n_heads = q_ref.shape[2] // V_DIM
    tq, tk = Q_TILE, K_TILE
    lam = (jnp.exp(jnp.sum(lq1_ref[...] * lk1_ref[...], keepdims=True))
           - jnp.exp(jnp.sum(lq2_ref[...] * lk2_ref[...], keepdims=True))
           + lam_init)
    out_gain = sub_ref[...] * (1.0 - lam_init)

    for h in range(n_heads):
        vx_scr[h, :, 0:V_DIM] = v_ref[0, :, h * V_DIM:(h + 1) * V_DIM]
        vx_scr[h, :, V_DIM:2 * V_DIM] = jnp.ones((s_len, V_DIM), _BF16)

    lane = lax.broadcasted_iota(jnp.int32, (tq, V_DIM), 1)
    row_id = lax.broadcasted_iota(jnp.int32, (tq, tk), 0)
    row_id = jnp.concatenate([row_id, row_id], axis=0)
    col_id = lax.broadcasted_iota(jnp.int32, (2 * tq, tk), 1)

    def lane_tiles(a):
        return [a[:, t:t + V7X_LANES] for t in range(0, a.shape[1], V7X_LANES)]

    tiles = [(h, qi) for h in range(n_heads)
             for qi in reversed(range(s_len // tq))]
    row_max = {}

    def scores(pos):
        (h, qi), slot = tiles[pos], pos % 2
        q_lo, cols = qi * tq, slice(h * V_DIM, (h + 1) * V_DIM)
        q = q_ref[0, q_lo:q_lo + tq, cols]
        zero = jnp.zeros_like(q)
        q2 = jnp.concatenate([jnp.where(lane < HEAD_DIM, q, zero),
                              jnp.where(lane >= HEAD_DIM, q, zero)], axis=0)
        m_run = jnp.full((2 * tq, V7X_LANES), NEG_INF, _F32)
        for k_lo in range(0, q_lo + tq, tk):
            s = _dot_nt(q2, k_ref[0, k_lo:k_lo + tk, cols])
            if k_lo + tk > q_lo:
                visible = (k_lo + col_id) <= (q_lo + row_id)
                s = jnp.where(visible, s, NEG_INF)
            s_scr[slot, k_lo // tk] = s
            for part in lane_tiles(s):
                m_run = jnp.maximum(m_run, part)
            yield
        row_max[pos] = jnp.max(m_run, axis=-1, keepdims=True)

    def probs(pos):
        (_, qi), slot = tiles[pos], pos % 2
        for k_lo in range(0, qi * tq + tq, tk):
            p = jnp.exp2(s_scr[slot, k_lo // tk] - row_max[pos])
            p_scr[slot, k_lo // tk] = p.astype(_BF16)
            yield

    def values(pos):
        (h, qi), slot = tiles[pos], pos % 2
        q_lo = qi * tq
        pv = jnp.zeros((2 * tq, 2 * V_DIM), _F32)
        for k_lo in range(0, q_lo + tq, tk):
            pv = pv + _dot(p_scr[slot, k_lo // tk], vx_scr[h, k_lo:k_lo + tk, :])
        maps = pv[:, 0:V_DIM] / pv[:, V_DIM:2 * V_DIM]
        o = maps[0:tq] - lam * maps[tq:2 * tq]
        o_ref[0, q_lo:q_lo + tq, h * V_DIM:(h + 1) * V_DIM] = (
            _rms(o, out_gain).astype(_BF16))

    for _ in scores(0):
        pass
    for pos in range(len(tiles) + 1):
        if pos >= 1:
            values(pos - 1)
        active = [stage(p) for stage, p in ((probs, pos), (scores, pos + 1))
                  if p < len(tiles)]
        while active:
            for gen in list(active):
                if next(gen, StopIteration) is StopIteration:
                    active.remove(gen)


def _attention(q, k, v, lq1, lk1, lq2, lk2, sub, lam_init):
    b, s, n = q.shape
    n_heads = n // V_DIM
    assert s % Q_TILE == 0 and s % K_TILE == 0 and n_heads % HEADS_PER_STEP == 0
    head = pl.BlockSpec((1, s, HEADS_PER_STEP * V_DIM), lambda i, h: (i, 0, h))
    vec = _const_spec((1, HEAD_DIM))
    return pl.pallas_call(
        functools.partial(_attn_kernel, lam_init=lam_init),
        grid=(b, n_heads // HEADS_PER_STEP),
        in_specs=[head, head, head, vec, vec, vec, vec, _const_spec((1, V_DIM))],
        out_specs=head,
        out_shape=jax.ShapeDtypeStruct((b, s, n), _BF16),
        scratch_shapes=[pltpu.VMEM((2, s // K_TILE, 2 * Q_TILE, K_TILE), _F32),
                        pltpu.VMEM((2, s // K_TILE, 2 * Q_TILE, K_TILE), _BF16),
                        pltpu.VMEM((HEADS_PER_STEP, s, 2 * V_DIM), _BF16)],
        compiler_params=pltpu.CompilerParams(
            dimension_semantics=("arbitrary", "arbitrary"),
            vmem_limit_bytes=ATTN_VMEM_LIMIT_BYTES),
        name="attention",
    )(q, k, v, lq1, lk1, lq2, lk2, sub)


def _out_mlp_kernel(x_ref, o_ref, wo_ref, gm_ref, wup_ref, wdn_ref, y_ref, *,
                    mlp_layer):
    x2 = x_ref[...] + _dot(o_ref[...], wo_ref[...])
    y_ref[...] = _mlp(x2, gm_ref[mlp_layer:mlp_layer + 1, :], wup_ref, wdn_ref)


def _out_mlp(x, o, w_o, g_mlp, mlp_layer, w_up, w_down):
    m, d = x.shape
    n = o.shape[1]
    d_ff = w_up.shape[2]
    tm = BIG_ROW_TILE
    assert m % tm == 0 and d_ff % FF_CHUNK == 0
    resident = pl.BlockSpec(memory_space=pltpu.VMEM)
    return pl.pallas_call(
        functools.partial(_out_mlp_kernel, mlp_layer=mlp_layer),
        grid=(m // tm,),
        in_specs=[pl.BlockSpec((tm, d), lambda i: (i, 0)),
                  pl.BlockSpec((tm, n), lambda i: (i, 0)),
                  resident, _const_spec(g_mlp.shape), resident, resident],
        out_specs=pl.BlockSpec((tm, d), lambda i: (i, 0)),
        out_shape=jax.ShapeDtypeStruct((m, d), _F32),
        compiler_params=pltpu.CompilerParams(
            dimension_semantics=("arbitrary",),
            vmem_limit_bytes=OUT_MLP_VMEM_LIMIT_BYTES),
        name="out_mlp",
    )(x, o, w_o.reshape(n, d), g_mlp, w_up.reshape(d, d_ff),
      w_down.reshape(d_ff, d))


def _lambda_init(layer_idx):
    return 0.8 - 0.6 * math.exp(-0.3 * layer_idx)


def kernel(x, a_norm, a_w_in, a_conv, a_w_out, kv_norm, w_kv, k_norm, b_norm,
           w_q, q_norm, lam_q1, lam_k1, lam_q2, lam_k2, sub_norm, w_o,
           mlp_norm, w_up, w_down):
    b, s, d = x.shape
    n_a, n_b = a_norm.shape[0], b_norm.shape[0]
    assert n_a == 1 and n_b == 1, "one mixer layer followed by one attention layer"
    n_qk = w_q.shape[2]
    assert w_kv.shape[1] == 2 * n_qk and n_qk % V7X_MXU_DIM == 0

    row = lambda g: g.reshape(1, -1)
    gid = np.arange(V7X_MXU_DIM) // HEAD_DIM
    ones = jnp.asarray(gid[:, None] == gid[None, :], dtype=_BF16)

    x, (w_up_bf0, w_down_bf0, w_kv_bf, w_q_bf) = _mixer(
        x, row(a_norm[0]), a_w_in, row(a_conv[0]), a_w_out, 0,
        [(w_up, 0), (w_down, 0), (w_kv[None], 0), (w_q, 0)])
    (x1, q, k, v), (w_o_bf, w_up_bf1, w_down_bf1) = _mlp_qkv(
        x.reshape(b * s, d), mlp_norm, 0, w_up_bf0, w_down_bf0,
        row(kv_norm), w_kv_bf, row(k_norm), row(b_norm[0]), w_q_bf,
        row(q_norm[0]), ones, [(w_o, 0), (w_up, 1), (w_down, 1)])
    o = _attention(q.reshape(b, s, n_qk), k.reshape(b, s, n_qk),
                   v.reshape(b, s, -1), row(lam_q1[0]), row(lam_k1[0]),
                   row(lam_q2[0]), row(lam_k2[0]), row(sub_norm[0]),
                   _lambda_init(n_a))
    y = _out_mlp(x1, o.reshape(b * s, -1), w_o_bf, mlp_norm, 1,
                 w_up_bf1, w_down_bf1)
    return y.reshape(b, s, d)
```

```python
import functools
import math

import jax
import jax.numpy as jnp
import numpy as np
from jax import lax
from jax.experimental import pallas as pl
from jax.experimental.pallas import tpu as pltpu

EPS = 1e-6
NEG_INF = -1e30
HEAD_DIM = 64
V_DIM = 2 * HEAD_DIM
CONV_WIDTH = 3

V7X_LANES = 128
V7X_SUBLANES = 8
V7X_MXU_DIM = 256
V7X_BF16_ROWS_PER_VREG = 16
V7X_VMEM_LIMIT_BYTES = 60 * 1024 * 1024
ATTN_VMEM_LIMIT_BYTES = 24 * 1024 * 1024
OUT_MLP_VMEM_LIMIT_BYTES = 34 * 1024 * 1024

ROW_TILE = 512
BIG_ROW_TILE = 1024
FF_CHUNK = 1024
Q_TILE = 256
K_TILE = 256
HEADS_PER_STEP = 2
CAST_ROWS = 128

_LOG2_E = math.log2(math.e)
_BF16 = jnp.bfloat16
_F32 = jnp.float32


def _dot(a, b):
    return jnp.dot(a, b, preferred_element_type=_F32)


def _dot_nt(a, b):
    return lax.dot_general(a, b, (((1,), (1,)), ((), ())),
                           preferred_element_type=_F32)


def _rms(x, g):
    ms = jnp.mean(x * x, axis=-1, keepdims=True)
    return x * lax.rsqrt(ms + EPS) * g


def _mlp(x, g, wup_ref, wdn_ref):
    h = _rms(x, g).astype(_BF16)
    acc = None
    d_ff = wup_ref.shape[1]
    for c in range(d_ff // FF_CHUNK):
        sl = slice(c * FF_CHUNK, (c + 1) * FF_CHUNK)
        a = jnp.maximum(_dot(h, wup_ref[:, sl]), 0.0)
        part = _dot((a * a).astype(_BF16), wdn_ref[sl, :])
        acc = part if acc is None else acc + part
    return x + acc


def _head_rms(y, ones_ref, gain):
    y2 = (y * y).astype(_BF16)
    n = y.shape[1]
    parts = [_dot(y2[:, t:t + V7X_MXU_DIM], ones_ref[...])
             for t in range(0, n, V7X_MXU_DIM)]
    ss = jnp.concatenate(parts, axis=1)
    gain_row = jnp.concatenate([gain] * (n // HEAD_DIM), axis=1)
    return y * lax.rsqrt(ss * (1.0 / HEAD_DIM) + EPS) * gain_row


def _const_spec(shape):
    return pl.BlockSpec(shape, lambda *_: (0,) * len(shape),
                        pipeline_mode=pl.Buffered(1))


def _layer_spec(w, layer):
    _, r, c = w.shape
    return pl.BlockSpec((None, r, c), lambda *_: (layer, 0, 0),
                        pipeline_mode=pl.Buffered(1))


def _cast_jobs(weights, n_steps, step_of):
    in_specs, out_specs, shapes = [], [], []
    for w, layer in weights:
        _, r, c = w.shape
        assert r % (n_steps * V7X_BF16_ROWS_PER_VREG) == 0
        block = (None, r // n_steps, c)
        in_specs.append(pl.BlockSpec(
            block, lambda *ids, layer=layer: (layer, step_of(*ids), 0)))
        out_specs.append(pl.BlockSpec(
            block, lambda *ids: (0, step_of(*ids), 0)))
        shapes.append(jax.ShapeDtypeStruct((1, r, c), _BF16))
    return in_specs, out_specs, shapes


def _run_casts(src_refs, dst_refs):
    for src, dst in zip(src_refs, dst_refs):
        dst[...] = src[...].astype(_BF16)


def _mixer_kernel(*refs, n_cast):
    x_ref, g_ref, win_ref, conv_ref, wout_ref = refs[:5]
    cast_src = refs[5:5 + n_cast]
    o_ref = refs[5 + n_cast]
    cast_dst = refs[6 + n_cast:6 + 2 * n_cast]
    carry_ref, win_bf, wout_bf = refs[6 + 2 * n_cast:]
    d = x_ref.shape[2]
    ts = x_ref.shape[1]

    @pl.when((pl.program_id(0) == 0) & (pl.program_id(1) == 0))
    def _():
        def cast_rows(i, _):
            r = pl.ds(pl.multiple_of(i * CAST_ROWS, CAST_ROWS), CAST_ROWS)
            win_bf[r, :] = win_ref[r, :].astype(_BF16)
            wout_bf[r, :] = wout_ref[r, :].astype(_BF16)
            return 0
        lax.fori_loop(0, d // CAST_ROWS, cast_rows, 0)

    @pl.when(pl.program_id(1) == 0)
    def _():
        carry_ref[...] = jnp.zeros_like(carry_ref)

    _run_casts(cast_src, cast_dst)

    x = x_ref[0]
    h = _rms(x, g_ref[...]).astype(_BF16)
    b_gate = _dot(h, win_bf[:, 0:d])
    c_gate = _dot(h, win_bf[:, d:2 * d])
    v = _dot(h, win_bf[:, 2 * d:3 * d])
    u = c_gate * v

    rows = lax.broadcasted_iota(jnp.int32, u.shape, 0)
    tail = carry_ref[...]
    prev1 = tail[V7X_SUBLANES - 1:V7X_SUBLANES]
    prev2 = tail[V7X_SUBLANES - 2:V7X_SUBLANES - 1]
    u1 = jnp.where(rows == 0, prev1, pltpu.roll(u, 1, 0))
    u2 = jnp.where(rows == 0, prev2,
                   jnp.where(rows == 1, prev1, pltpu.roll(u, 2, 0)))
    w = conv_ref[...]
    y = w[:, 0:d] * u2 + w[:, d:2 * d] * u1 + w[:, 2 * d:3 * d] * u
    carry_ref[...] = u[ts - V7X_SUBLANES:ts]

    o_ref[0] = x + _dot((b_gate * y).astype(_BF16), wout_bf[...])


def _mixer(x, g, w_in, conv_w, w_out, layer, next_weights):
    b, s, d = x.shape
    ts = BIG_ROW_TILE
    assert s % ts == 0 and d % CAST_ROWS == 0
    n_seq = s // ts
    tok = pl.BlockSpec((1, ts, d), lambda i, j: (i, j, 0))
    cast_in, cast_out, cast_shapes = _cast_jobs(next_weights, b * n_seq,
                                                lambda i, j: i * n_seq + j)
    outs = pl.pallas_call(
        functools.partial(_mixer_kernel, n_cast=len(cast_in)),
        grid=(b, n_seq),
        in_specs=[tok, _const_spec((1, d)), _layer_spec(w_in, layer),
                  _const_spec((1, CONV_WIDTH * d)), _layer_spec(w_out, layer),
                  *cast_in],
        out_specs=[tok, *cast_out],
        out_shape=[jax.ShapeDtypeStruct(x.shape, _F32), *cast_shapes],
        scratch_shapes=[pltpu.VMEM((V7X_SUBLANES, d), _F32),
                        pltpu.VMEM(w_in.shape[1:], _BF16),
                        pltpu.VMEM(w_out.shape[1:], _BF16)],
        compiler_params=pltpu.CompilerParams(
            dimension_semantics=("arbitrary", "arbitrary"),
            vmem_limit_bytes=V7X_VMEM_LIMIT_BYTES),
        name="mixer",
    )(x, g, w_in, conv_w, w_out, *[w for w, _ in next_weights])
    return outs[0], outs[1:]


def _mlp_qkv_kernel(*refs, n_cast, mlp_layer):
    (x_ref, gm_ref, wup_ref, wdn_ref, gkv_ref, wkv_ref, gk_ref, gb_ref, wq_ref,
     gq_ref, ones_ref) = refs[:11]
    cast_src = refs[11:11 + n_cast]
    x1_ref, q_ref, k_ref, v_ref = refs[11 + n_cast:15 + n_cast]
    cast_dst = refs[15 + n_cast:]

    _run_casts(cast_src, cast_dst)

    x1 = _mlp(x_ref[...], gm_ref[mlp_layer:mlp_layer + 1, :], wup_ref, wdn_ref)
    x1_ref[...] = x1

    nk = k_ref.shape[1]
    hkv = _rms(x1, gkv_ref[...]).astype(_BF16)
    k = _dot(hkv, wkv_ref[:, 0:nk])
    k_ref[...] = _head_rms(k, ones_ref, gk_ref[...]).astype(_BF16)
    v_ref[...] = _dot(hkv, wkv_ref[:, nk:]).astype(_BF16)

    hq = _rms(x1, gb_ref[...]).astype(_BF16)
    q = _head_rms(_dot(hq, wq_ref[...]), ones_ref, gq_ref[...])
    q_ref[...] = (q * (HEAD_DIM ** -0.5 * _LOG2_E)).astype(_BF16)


def _mlp_qkv(x, g_mlp, mlp_layer, w_up, w_down, g_kv, w_kv, g_k, g_b, w_q, g_q,
             ones, next_weights):
    m, d = x.shape
    d_ff = w_up.shape[2]
    nq = w_q.shape[2]
    nkv = w_kv.shape[2]
    nk = nq
    tm = ROW_TILE
    assert m % tm == 0 and d_ff % FF_CHUNK == 0

    def rows(n):
        return pl.BlockSpec((tm, n), lambda i: (i, 0))

    cast_in, cast_out, cast_shapes = _cast_jobs(next_weights, m // tm,
                                                lambda i: i)
    outs = pl.pallas_call(
        functools.partial(_mlp_qkv_kernel, n_cast=len(cast_in),
                          mlp_layer=mlp_layer),
        grid=(m // tm,),
        in_specs=[rows(d), _const_spec(g_mlp.shape), _layer_spec(w_up, 0),
                  _layer_spec(w_down, 0), _const_spec((1, d)),
                  _layer_spec(w_kv, 0), _const_spec((1, HEAD_DIM)),
                  _const_spec((1, d)), _layer_spec(w_q, 0),
                  _const_spec((1, HEAD_DIM)),
                  _const_spec((V7X_MXU_DIM, V7X_MXU_DIM)), *cast_in],
        out_specs=[rows(d), rows(nq), rows(nk), rows(nkv - nk), *cast_out],
        out_shape=[jax.ShapeDtypeStruct((m, d), _F32),
                   jax.ShapeDtypeStruct((m, nq), _BF16),
                   jax.ShapeDtypeStruct((m, nk), _BF16),
                   jax.ShapeDtypeStruct((m, nkv - nk), _BF16), *cast_shapes],
        compiler_params=pltpu.CompilerParams(
            dimension_semantics=("arbitrary",),
            vmem_limit_bytes=V7X_VMEM_LIMIT_BYTES),
        name="mlp_qkv",
    )(x, g_mlp, w_up, w_down, g_kv, w_kv, g_k, g_b, w_q, g_q, ones,
      *[w for w, _ in next_weights])
    return outs[:4], outs[4:]


def _attn_kernel(q_ref, k_ref, v_ref, lq1_ref, lk1_ref, lq2_ref, lk2_ref,
                 sub_ref, o_ref, s_scr, p_scr, vx_scr, *, lam_init):
    s_len = q_ref.shape[1]
    n_heads = q_ref.shape[2] // V_DIM
    tq, tk = Q_TILE, K_TILE
    lam = (jnp.exp(jnp.sum(lq1_ref[...] * lk1_ref[...], keepdims=True))
           - jnp.exp(jnp.sum(lq2_ref[...] * lk2_ref[...], keepdims=True))
           + lam_init)
    out_gain = sub_ref[...] * (1.0 - lam_init)

    for h in range(n_heads):
        vx_scr[h, :, 0:V_DIM] = v_ref[0, :, h * V_DIM:(h + 1) * V_DIM]
        vx_scr[h, :, V_DIM:2 * V_DIM] = jnp.ones((s_len, V_DIM), _BF16)

    lane = lax.broadcasted_iota(jnp.int32, (tq, V_DIM), 1)
    row_id = lax.broadcasted_iota(jnp.int32, (tq, tk), 0)
    row_id = jnp.concatenate([row_id, row_id], axis=0)
    col_id = lax.broadcasted_iota(jnp.int32, (2 * tq, tk), 1)

    def lane_tiles(a):
        return [a[:, t:t + V7X_LANES] for t in range(0, a.shape[1], V7X_LANES)]

    tiles = [(h, qi) for h in range(n_heads)
             for qi in reversed(range(s_len // tq))]
    row_max = {}

    def scores(pos):
        (h, qi), slot = tiles[pos], pos % 2
        q_lo, cols = qi * tq, slice(h * V_DIM, (h + 1) * V_DIM)
        q = q_ref[0, q_lo:q_lo + tq, cols]
        zero = jnp.zeros_like(q)
        q2 = jnp.concatenate([jnp.where(lane < HEAD_DIM, q, zero),
                              jnp.where(lane >= HEAD_DIM, q, zero)], axis=0)
        m_run = jnp.full((2 * tq, V7X_LANES), NEG_INF, _F32)
        for k_lo in range(0, q_lo + tq, tk):
            s = _dot_nt(q2, k_ref[0, k_lo:k_lo + tk, cols])
            if k_lo + tk > q_lo:
                visible = (k_lo + col_id) <= (q_lo + row_id)
                s = jnp.where(visible, s, NEG_INF)
            s_scr[slot, :, k_lo:k_lo + tk] = s
            for part in lane_tiles(s):
                m_run = jnp.maximum(m_run, part)
            yield
        row_max[pos] = jnp.max(m_run, axis=-1, keepdims=True)

    def probs(pos):
        (_, qi), slot = tiles[pos], pos % 2
        for k_lo in range(0, qi * tq + tq, tk):
            p = jnp.exp2(s_scr[slot, :, k_lo:k_lo + tk] - row_max[pos])
            p_scr[slot, :, k_lo:k_lo + tk] = p.astype(_BF16)
            yield

    def values(pos):
        (h, qi), slot = tiles[pos], pos % 2
        q_lo, n_keys = qi * tq, qi * tq + tq
        pv = _dot(p_scr[slot, :, 0:n_keys], vx_scr[h, 0:n_keys, :])
        maps = pv[:, 0:V_DIM] / pv[:, V_DIM:2 * V_DIM]
        o = maps[0:tq] - lam * maps[tq:2 * tq]
        o_ref[0, q_lo:q_lo + tq, h * V_DIM:(h + 1) * V_DIM] = (
            _rms(o, out_gain).astype(_BF16))

    for _ in scores(0):
        pass
    for pos in range(len(tiles) + 1):
        if pos >= 1:
            values(pos - 1)
        active = [stage(p) for stage, p in ((probs, pos), (scores, pos + 1))
                  if p < len(tiles)]
        while active:
            for gen in list(active):
                if next(gen, StopIteration) is StopIteration:
                    active.remove(gen)


def _attention(q, k, v, lq1, lk1, lq2, lk2, sub, lam_init):
    b, s, n = q.shape
    n_heads = n // V_DIM
    assert s % Q_TILE == 0 and s % K_TILE == 0 and n_heads % HEADS_PER_STEP == 0
    head = pl.BlockSpec((1, s, HEADS_PER_STEP * V_DIM), lambda i, h: (i, 0, h))
    vec = _const_spec((1, HEAD_DIM))
    return pl.pallas_call(
        functools.partial(_attn_kernel, lam_init=lam_init),
        grid=(b, n_heads // HEADS_PER_STEP),
        in_specs=[head, head, head, vec, vec, vec, vec, _const_spec((1, V_DIM))],
        out_specs=head,
        out_shape=jax.ShapeDtypeStruct((b, s, n), _BF16),
        scratch_shapes=[pltpu.VMEM((2, 2 * Q_TILE, s), _F32),
                        pltpu.VMEM((2, 2 * Q_TILE, s), _BF16),
                        pltpu.VMEM((HEADS_PER_STEP, s, 2 * V_DIM), _BF16)],
        compiler_params=pltpu.CompilerParams(
            dimension_semantics=("arbitrary", "arbitrary"),
            vmem_limit_bytes=ATTN_VMEM_LIMIT_BYTES),
        name="attention",
    )(q, k, v, lq1, lk1, lq2, lk2, sub)


def _out_mlp_kernel(x_ref, o_ref, wo_ref, gm_ref, wup_ref, wdn_ref, y_ref, *,
                    mlp_layer):
    x2 = x_ref[...] + _dot(o_ref[...], wo_ref[...])
    y_ref[...] = _mlp(x2, gm_ref[mlp_layer:mlp_layer + 1, :], wup_ref, wdn_ref)


def _out_mlp(x, o, w_o, g_mlp, mlp_layer, w_up, w_down):
    m, d = x.shape
    n = o.shape[1]
    d_ff = w_up.shape[2]
    tm = BIG_ROW_TILE
    assert m % tm == 0 and d_ff % FF_CHUNK == 0
    resident = pl.BlockSpec(memory_space=pltpu.VMEM)
    return pl.pallas_call(
        functools.partial(_out_mlp_kernel, mlp_layer=mlp_layer),
        grid=(m // tm,),
        in_specs=[pl.BlockSpec((tm, d), lambda i: (i, 0)),
                  pl.BlockSpec((tm, n), lambda i: (i, 0)),
                  resident, _const_spec(g_mlp.shape), resident, resident],
        out_specs=pl.BlockSpec((tm, d), lambda i: (i, 0)),
        out_shape=jax.ShapeDtypeStruct((m, d), _F32),
        compiler_params=pltpu.CompilerParams(
            dimension_semantics=("arbitrary",),
            vmem_limit_bytes=OUT_MLP_VMEM_LIMIT_BYTES),
        name="out_mlp",
    )(x, o, w_o.reshape(n, d), g_mlp, w_up.reshape(d, d_ff),
      w_down.reshape(d_ff, d))


def _lambda_init(layer_idx):
    return 0.8 - 0.6 * math.exp(-0.3 * layer_idx)


def kernel(x, a_norm, a_w_in, a_conv, a_w_out, kv_norm, w_kv, k_norm, b_norm,
           w_q, q_norm, lam_q1, lam_k1, lam_q2, lam_k2, sub_norm, w_o,
           mlp_norm, w_up, w_down):
    b, s, d = x.shape
    n_a, n_b = a_norm.shape[0], b_norm.shape[0]
    assert n_a == 1 and n_b == 1, "one mixer layer followed by one attention layer"
    n_qk = w_q.shape[2]
    assert w_kv.shape[1] == 2 * n_qk and n_qk % V7X_MXU_DIM == 0

    row = lambda g: g.reshape(1, -1)
    gid = np.arange(V7X_MXU_DIM) // HEAD_DIM
    ones = jnp.asarray(gid[:, None] == gid[None, :], dtype=_BF16)

    x, (w_up_bf0, w_down_bf0, w_kv_bf, w_q_bf) = _mixer(
        x, row(a_norm[0]), a_w_in, row(a_conv[0]), a_w_out, 0,
        [(w_up, 0), (w_down, 0), (w_kv[None], 0), (w_q, 0)])
    (x1, q, k, v), (w_o_bf, w_up_bf1, w_down_bf1) = _mlp_qkv(
        x.reshape(b * s, d), mlp_norm, 0, w_up_bf0, w_down_bf0,
        row(kv_norm), w_kv_bf, row(k_norm), row(b_norm[0]), w_q_bf,
        row(q_norm[0]), ones, [(w_o, 0), (w_up, 1), (w_down, 1)])
    o = _attention(q.reshape(b, s, n_qk), k.reshape(b, s, n_qk),
                   v.reshape(b, s, -1), row(lam_q1[0]), row(lam_k1[0]),
                   row(lam_q2[0]), row(lam_k2[0]), row(sub_norm[0]),
                   _lambda_init(n_a))
    y = _out_mlp(x1, o.reshape(b * s, -1), w_o_bf, mlp_norm, 1,
                 w_up_bf1, w_down_bf1)
    return y.reshape(b, s, d)
```

```python
import functools
import math

import jax
import jax.numpy as jnp
import numpy as np
from jax import lax
from jax.experimental import pallas as pl
from jax.experimental.pallas import tpu as pltpu

EPS = 1e-6
NEG_INF = -1e30
HEAD_DIM = 64
V_DIM = 2 * HEAD_DIM
CONV_WIDTH = 3

V7X_LANES = 128
V7X_SUBLANES = 8
V7X_MXU_DIM = 256
V7X_BF16_ROWS_PER_VREG = 16
V7X_VMEM_LIMIT_BYTES = 60 * 1024 * 1024
ATTN_VMEM_LIMIT_BYTES = 24 * 1024 * 1024
OUT_MLP_VMEM_LIMIT_BYTES = 34 * 1024 * 1024

ROW_TILE = 512
BIG_ROW_TILE = 1024
FF_CHUNK = 1024
Q_TILE = 256
K_TILE = 256
HEADS_PER_STEP = 2
CAST_ROWS = 128

_LOG2_E = math.log2(math.e)
_BF16 = jnp.bfloat16
_F32 = jnp.float32


def _dot(a, b):
    return jnp.dot(a, b, preferred_element_type=_F32)


def _dot_nt(a, b):
    return lax.dot_general(a, b, (((1,), (1,)), ((), ())),
                           preferred_element_type=_F32)


def _rms(x, g):
    ms = jnp.mean(x * x, axis=-1, keepdims=True)
    return x * lax.rsqrt(ms + EPS) * g


def _mlp(x, g, wup_ref, wdn_ref):
    h = _rms(x, g).astype(_BF16)
    acc = x
    d_ff = wup_ref.shape[1]
    for c in range(d_ff // FF_CHUNK):
        sl = slice(c * FF_CHUNK, (c + 1) * FF_CHUNK)
        a = jnp.maximum(_dot(h, wup_ref[:, sl]), 0.0)
        acc = acc + _dot((a * a).astype(_BF16), wdn_ref[sl, :])
    return acc


def _head_rms(y, ones_ref, gain):
    y2 = (y * y).astype(_BF16)
    n = y.shape[1]
    parts = [_dot(y2[:, t:t + V7X_MXU_DIM], ones_ref[...])
             for t in range(0, n, V7X_MXU_DIM)]
    ss = jnp.concatenate(parts, axis=1)
    gain_row = jnp.concatenate([gain] * (n // HEAD_DIM), axis=1)
    return y * lax.rsqrt(ss * (1.0 / HEAD_DIM) + EPS) * gain_row


def _const_spec(shape):
    return pl.BlockSpec(shape, lambda *_: (0,) * len(shape),
                        pipeline_mode=pl.Buffered(1))


def _layer_spec(w, layer):
    _, r, c = w.shape
    return pl.BlockSpec((None, r, c), lambda *_: (layer, 0, 0),
                        pipeline_mode=pl.Buffered(1))


def _cast_jobs(weights, n_steps, step_of):
    in_specs, out_specs, shapes = [], [], []
    for w, layer in weights:
        _, r, c = w.shape
        assert r % (n_steps * V7X_BF16_ROWS_PER_VREG) == 0
        block = (None, r // n_steps, c)
        in_specs.append(pl.BlockSpec(
            block, lambda *ids, layer=layer: (layer, step_of(*ids), 0)))
        out_specs.append(pl.BlockSpec(
            block, lambda *ids: (0, step_of(*ids), 0)))
        shapes.append(jax.ShapeDtypeStruct((1, r, c), _BF16))
    return in_specs, out_specs, shapes


def _run_casts(src_refs, dst_refs):
    for src, dst in zip(src_refs, dst_refs):
        dst[...] = src[...].astype(_BF16)


def _mixer_kernel(*refs, n_cast):
    x_ref, g_ref, win_ref, conv_ref, wout_ref = refs[:5]
    cast_src = refs[5:5 + n_cast]
    o_ref = refs[5 + n_cast]
    cast_dst = refs[6 + n_cast:6 + 2 * n_cast]
    carry_ref, win_bf, wout_bf = refs[6 + 2 * n_cast:]
    d = x_ref.shape[2]
    ts = x_ref.shape[1]

    @pl.when((pl.program_id(0) == 0) & (pl.program_id(1) == 0))
    def _():
        def cast_rows(i, _):
            r = pl.ds(pl.multiple_of(i * CAST_ROWS, CAST_ROWS), CAST_ROWS)
            win_bf[r, :] = win_ref[r, :].astype(_BF16)
            wout_bf[r, :] = wout_ref[r, :].astype(_BF16)
            return 0
        lax.fori_loop(0, d // CAST_ROWS, cast_rows, 0)

    @pl.when(pl.program_id(1) == 0)
    def _():
        carry_ref[...] = jnp.zeros_like(carry_ref)

    _run_casts(cast_src, cast_dst)

    x = x_ref[0]
    h = _rms(x, g_ref[...]).astype(_BF16)
    b_gate = _dot(h, win_bf[:, 0:d])
    c_gate = _dot(h, win_bf[:, d:2 * d])
    v = _dot(h, win_bf[:, 2 * d:3 * d])
    u = c_gate * v

    rows = lax.broadcasted_iota(jnp.int32, u.shape, 0)
    tail = carry_ref[...]
    prev1 = tail[V7X_SUBLANES - 1:V7X_SUBLANES]
    prev2 = tail[V7X_SUBLANES - 2:V7X_SUBLANES - 1]
    u1 = jnp.where(rows == 0, prev1, pltpu.roll(u, 1, 0))
    u2 = jnp.where(rows == 0, prev2,
                   jnp.where(rows == 1, prev1, pltpu.roll(u, 2, 0)))
    w = conv_ref[...]
    y = w[:, 0:d] * u2 + w[:, d:2 * d] * u1 + w[:, 2 * d:3 * d] * u
    carry_ref[...] = u[ts - V7X_SUBLANES:ts]

    o_ref[0] = x + _dot((b_gate * y).astype(_BF16), wout_bf[...])


def _mixer(x, g, w_in, conv_w, w_out, layer, next_weights):
    b, s, d = x.shape
    ts = BIG_ROW_TILE
    assert s % ts == 0 and d % CAST_ROWS == 0
    n_seq = s // ts
    tok = pl.BlockSpec((1, ts, d), lambda i, j: (i, j, 0))
    cast_in, cast_out, cast_shapes = _cast_jobs(next_weights, b * n_seq,
                                                lambda i, j: i * n_seq + j)
    outs = pl.pallas_call(
        functools.partial(_mixer_kernel, n_cast=len(cast_in)),
        grid=(b, n_seq),
        in_specs=[tok, _const_spec((1, d)), _layer_spec(w_in, layer),
                  _const_spec((1, CONV_WIDTH * d)), _layer_spec(w_out, layer),
                  *cast_in],
        out_specs=[tok, *cast_out],
        out_shape=[jax.ShapeDtypeStruct(x.shape, _F32), *cast_shapes],
        scratch_shapes=[pltpu.VMEM((V7X_SUBLANES, d), _F32),
                        pltpu.VMEM(w_in.shape[1:], _BF16),
                        pltpu.VMEM(w_out.shape[1:], _BF16)],
        compiler_params=pltpu.CompilerParams(
            dimension_semantics=("arbitrary", "arbitrary"),
            vmem_limit_bytes=V7X_VMEM_LIMIT_BYTES),
        name="mixer",
    )(x, g, w_in, conv_w, w_out, *[w for w, _ in next_weights])
    return outs[0], outs[1:]


def _mlp_qkv_kernel(*refs, n_cast, mlp_layer):
    (x_ref, gm_ref, wup_ref, wdn_ref, gkv_ref, wkv_ref, gk_ref, gb_ref, wq_ref,
     gq_ref, ones_ref) = refs[:11]
    cast_src = refs[11:11 + n_cast]
    x1_ref, q_ref, k_ref, v_ref = refs[11 + n_cast:15 + n_cast]
    cast_dst = refs[15 + n_cast:]

    _run_casts(cast_src, cast_dst)

    x1 = _mlp(x_ref[...], gm_ref[mlp_layer:mlp_layer + 1, :], wup_ref, wdn_ref)
    x1_ref[...] = x1

    nk = k_ref.shape[1]
    hkv = _rms(x1, gkv_ref[...]).astype(_BF16)
    k = _dot(hkv, wkv_ref[:, 0:nk])
    k_ref[...] = _head_rms(k, ones_ref, gk_ref[...]).astype(_BF16)
    v_ref[...] = _dot(hkv, wkv_ref[:, nk:]).astype(_BF16)

    hq = _rms(x1, gb_ref[...]).astype(_BF16)
    q = _head_rms(_dot(hq, wq_ref[...]), ones_ref, gq_ref[...])
    q_ref[...] = (q * (HEAD_DIM ** -0.5 * _LOG2_E)).astype(_BF16)


def _mlp_qkv(x, g_mlp, mlp_layer, w_up, w_down, g_kv, w_kv, g_k, g_b, w_q, g_q,
             ones, next_weights):
    m, d = x.shape
    d_ff = w_up.shape[2]
    nq = w_q.shape[2]
    nkv = w_kv.shape[2]
    nk = nq
    tm = ROW_TILE
    assert m % tm == 0 and d_ff % FF_CHUNK == 0

    def rows(n):
        return pl.BlockSpec((tm, n), lambda i: (i, 0))

    cast_in, cast_out, cast_shapes = _cast_jobs(next_weights, m // tm,
                                                lambda i: i)
    outs = pl.pallas_call(
        functools.partial(_mlp_qkv_kernel, n_cast=len(cast_in),
                          mlp_layer=mlp_layer),
        grid=(m // tm,),
        in_specs=[rows(d), _const_spec(g_mlp.shape), _layer_spec(w_up, 0),
                  _layer_spec(w_down, 0), _const_spec((1, d)),
                  _layer_spec(w_kv, 0), _const_spec((1, HEAD_DIM)),
                  _const_spec((1, d)), _layer_spec(w_q, 0),
                  _const_spec((1, HEAD_DIM)),
                  _const_spec((V7X_MXU_DIM, V7X_MXU_DIM)), *cast_in],
        out_specs=[rows(d), rows(nq), rows(nk), rows(nkv - nk), *cast_out],
        out_shape=[jax.ShapeDtypeStruct((m, d), _F32),
                   jax.ShapeDtypeStruct((m, nq), _BF16),
                   jax.ShapeDtypeStruct((m, nk), _BF16),
                   jax.ShapeDtypeStruct((m, nkv - nk), _BF16), *cast_shapes],
        compiler_params=pltpu.CompilerParams(
            dimension_semantics=("arbitrary",),
            vmem_limit_bytes=V7X_VMEM_LIMIT_BYTES),
        name="mlp_qkv",
    )(x, g_mlp, w_up, w_down, g_kv, w_kv, g_k, g_b, w_q, g_q, ones,
      *[w for w, _ in next_weights])
    return outs[:4], outs[4:]


def _attn_kernel(q_ref, k_ref, v_ref, lq1_ref, lk1_ref, lq2_ref, lk2_ref,
                 sub_ref, o_ref, s_scr, p_scr, vx_scr, *, lam_init):
    s_len = q_ref.shape[1]
    n_heads = q_ref.shape[2] // V_DIM
    tq, tk = Q_TILE, K_TILE
    lam = (jnp.exp(jnp.sum(lq1_ref[...] * lk1_ref[...], keepdims=True))
           - jnp.exp(jnp.sum(lq2_ref[...] * lk2_ref[...], keepdims=True))
           + lam_init)
    out_gain = sub_ref[...] * (1.0 - lam_init)

    for h in range(n_heads):
        vx_scr[h, :, 0:V_DIM] = v_ref[0, :, h * V_DIM:(h + 1) * V_DIM]
        vx_scr[h, :, V_DIM:2 * V_DIM] = jnp.ones((s_len, V_DIM), _BF16)

    lane = lax.broadcasted_iota(jnp.int32, (tq, V_DIM), 1)
    row_id = lax.broadcasted_iota(jnp.int32, (tq, tk), 0)
    row_id = jnp.concatenate([row_id, row_id], axis=0)
    col_id = lax.broadcasted_iota(jnp.int32, (2 * tq, tk), 1)

    def lane_tiles(a):
        return [a[:, t:t + V7X_LANES] for t in range(0, a.shape[1], V7X_LANES)]

    tiles = [(h, qi) for h in range(n_heads)
             for qi in reversed(range(s_len // tq))]
    row_max = {}

    def scores(pos):
        (h, qi), slot = tiles[pos], pos % 2
        q_lo, cols = qi * tq, slice(h * V_DIM, (h + 1) * V_DIM)
        q = q_ref[0, q_lo:q_lo + tq, cols]
        zero = jnp.zeros_like(q)
        q2 = jnp.concatenate([jnp.where(lane < HEAD_DIM, q, zero),
                              jnp.where(lane >= HEAD_DIM, q, zero)], axis=0)
        m_run = jnp.full((2 * tq, V7X_LANES), NEG_INF, _F32)
        for k_lo in range(0, q_lo + tq, tk):
            s = _dot_nt(q2, k_ref[0, k_lo:k_lo + tk, cols])
            if k_lo + tk > q_lo:
                row_id = lax.broadcasted_iota(jnp.int32, (2 * tq, tk), 0) & (tq - 1)
                col_id = lax.broadcasted_iota(jnp.int32, (2 * tq, tk), 1)
                visible = (k_lo + col_id) <= (q_lo + row_id)
                s = jnp.where(visible, s, NEG_INF)
            s_scr[slot, :, k_lo:k_lo + tk] = s
            for part in lane_tiles(s):
                m_run = jnp.maximum(m_run, part)
            yield
        row_max[pos] = jnp.max(m_run, axis=-1, keepdims=True)

    def probs(pos):
        (_, qi), slot = tiles[pos], pos % 2
        for k_lo in range(0, qi * tq + tq, tk):
            p = jnp.exp2(s_scr[slot, :, k_lo:k_lo + tk] - row_max[pos])
            p_scr[slot, :, k_lo:k_lo + tk] = p.astype(_BF16)
            yield

    def values(pos):
        (h, qi), slot = tiles[pos], pos % 2
        q_lo, n_keys = qi * tq, qi * tq + tq
        pv = _dot(p_scr[slot, :, 0:n_keys], vx_scr[h, 0:n_keys, :])
        maps = pv[:, 0:V_DIM] / pv[:, V_DIM:2 * V_DIM]
        o = maps[0:tq] - lam * maps[tq:2 * tq]
        o_ref[0, q_lo:q_lo + tq, h * V_DIM:(h + 1) * V_DIM] = (
            _rms(o, out_gain).astype(_BF16))

    for _ in scores(0):
        pass
    for pos in range(len(tiles) + 1):
        if pos >= 1:
            values(pos - 1)
        active = [stage(p) for stage, p in ((probs, pos), (scores, pos + 1))
                  if p < len(tiles)]
        while active:
            for gen in list(active):
                if next(gen, StopIteration) is StopIteration:
                    active.remove(gen)


def _attention(q, k, v, lq1, lk1, lq2, lk2, sub, lam_init):
    b, s, n = q.shape
    n_heads = n // V_DIM
    assert s % Q_TILE == 0 and s % K_TILE == 0 and n_heads % HEADS_PER_STEP == 0
    head = pl.BlockSpec((1, s, HEADS_PER_STEP * V_DIM), lambda i, h: (i, 0, h))
    vec = _const_spec((1, HEAD_DIM))
    return pl.pallas_call(
        functools.partial(_attn_kernel, lam_init=lam_init),
        grid=(b, n_heads // HEADS_PER_STEP),
        in_specs=[head, head, head, vec, vec, vec, vec, _const_spec((1, V_DIM))],
        out_specs=head,
        out_shape=jax.ShapeDtypeStruct((b, s, n), _BF16),
        scratch_shapes=[pltpu.VMEM((2, 2 * Q_TILE, s), _F32),
                        pltpu.VMEM((2, 2 * Q_TILE, s), _BF16),
                        pltpu.VMEM((HEADS_PER_STEP, s, 2 * V_DIM), _BF16)],
        compiler_params=pltpu.CompilerParams(
            dimension_semantics=("arbitrary", "arbitrary"),
            vmem_limit_bytes=ATTN_VMEM_LIMIT_BYTES),
        name="attention",
    )(q, k, v, lq1, lk1, lq2, lk2, sub)


def _out_mlp_kernel(x_ref, o_ref, wo_ref, gm_ref, wup_ref, wdn_ref, y_ref, *,
                    mlp_layer):
    x2 = x_ref[...] + _dot(o_ref[...], wo_ref[...])
    y_ref[...] = _mlp(x2, gm_ref[mlp_layer:mlp_layer + 1, :], wup_ref, wdn_ref)


def _out_mlp(x, o, w_o, g_mlp, mlp_layer, w_up, w_down):
    m, d = x.shape
    n = o.shape[1]
    d_ff = w_up.shape[2]
    tm = BIG_ROW_TILE
    assert m % tm == 0 and d_ff % FF_CHUNK == 0
    resident = pl.BlockSpec(memory_space=pltpu.VMEM)
    return pl.pallas_call(
        functools.partial(_out_mlp_kernel, mlp_layer=mlp_layer),
        grid=(m // tm,),
        in_specs=[pl.BlockSpec((tm, d), lambda i: (i, 0)),
                  pl.BlockSpec((tm, n), lambda i: (i, 0)),
                  resident, _const_spec(g_mlp.shape), resident, resident],
        out_specs=pl.BlockSpec((tm, d), lambda i: (i, 0)),
        out_shape=jax.ShapeDtypeStruct((m, d), _F32),
        compiler_params=pltpu.CompilerParams(
            dimension_semantics=("arbitrary",),
            vmem_limit_bytes=OUT_MLP_VMEM_LIMIT_BYTES),
        name="out_mlp",
    )(x, o, w_o.reshape(n, d), g_mlp, w_up.reshape(d, d_ff),
      w_down.reshape(d_ff, d))


def _lambda_init(layer_idx):
    return 0.8 - 0.6 * math.exp(-0.3 * layer_idx)


def kernel(x, a_norm, a_w_in, a_conv, a_w_out, kv_norm, w_kv, k_norm, b_norm,
           w_q, q_norm, lam_q1, lam_k1, lam_q2, lam_k2, sub_norm, w_o,
           mlp_norm, w_up, w_down):
    b, s, d = x.shape
    n_a, n_b = a_norm.shape[0], b_norm.shape[0]
    assert n_a == 1 and n_b == 1, "one mixer layer followed by one attention layer"
    n_qk = w_q.shape[2]
    assert w_kv.shape[1] == 2 * n_qk and n_qk % V7X_MXU_DIM == 0

    row = lambda g: g.reshape(1, -1)
    gid = np.arange(V7X_MXU_DIM) // HEAD_DIM
    ones = jnp.asarray(gid[:, None] == gid[None, :], dtype=_BF16)

    x, (w_up_bf0, w_down_bf0, w_kv_bf, w_q_bf) = _mixer(
        x, row(a_norm[0]), a_w_in, row(a_conv[0]), a_w_out, 0,
        [(w_up, 0), (w_down, 0), (w_kv[None], 0), (w_q, 0)])
    (x1, q, k, v), (w_o_bf, w_up_bf1, w_down_bf1) = _mlp_qkv(
        x.reshape(b * s, d), mlp_norm, 0, w_up_bf0, w_down_bf0,
        row(kv_norm), w_kv_bf, row(k_norm), row(b_norm[0]), w_q_bf,
        row(q_norm[0]), ones, [(w_o, 0), (w_up, 1), (w_down, 1)])
    o = _attention(q.reshape(b, s, n_qk), k.reshape(b, s, n_qk),
                   v.reshape(b, s, -1), row(lam_q1[0]), row(lam_k1[0]),
                   row(lam_q2[0]), row(lam_k2[0]), row(sub_norm[0]),
                   _lambda_init(n_a))
    y = _out_mlp(x1, o.reshape(b * s, -1), w_o_bf, mlp_norm, 1,
                 w_up_bf1, w_down_bf1)
    return y.reshape(b, s, d)
```
